```python
import jax, jax.numpy as jnp
from jax import lax
import numpy as np

D_MODEL = 1024
BATCH = 16
SEQ = 2048
DEPTH = 2

GRID_W = 64
CTX_LEN = 256
EPS = 1e-6
NEG_INF = -1e30
MIX_WIDTH = D_MODEL
A_GROUPS = 4
A_CHUNK = 128
A_WIDTH = MIX_WIDTH // 2
A_GDIM = A_WIDTH // A_GROUPS
B_HEADS = 8
B_WIDTH = MIX_WIDTH - A_WIDTH
B_HDIM = B_WIDTH // B_HEADS
NA_ROWS = 8
NA_COLS = 16
NA_QCB = NA_COLS
NA_KCB = 2 * NA_COLS
C_HEADS = 16
C_KV_HEADS = 4
C_HDIM = D_MODEL // C_HEADS
C_QBLOCK = 128
ROPE_THETA = 10000.0
D_FF = 2816
CONV_W = 3
N_EVEN = (DEPTH + 1) // 2
N_ODD = DEPTH // 2
IN_AB = 2 * A_WIDTH + 3 * B_WIDTH
IN_C = (C_HEADS + 2 * C_KV_HEADS) * C_HDIM

kernel_name = "hybrid_gmlp_natten_gqa_convffn_dit"


def rms_norm(x, g):
    xf = x.astype(jnp.float32)
    y = xf * lax.rsqrt(jnp.mean(xf * xf, axis=-1, keepdims=True) + EPS)
    return (y * g.astype(jnp.float32)).astype(x.dtype)


def layer_norm(x, g):
    xf = x.astype(jnp.float32)
    mu = jnp.mean(xf, axis=-1, keepdims=True)
    var = jnp.mean(jnp.square(xf - mu), axis=-1, keepdims=True)
    return ((xf - mu) * lax.rsqrt(var + EPS) * g.astype(jnp.float32)).astype(x.dtype)


def ada_chunks(cvec, w, b):
    m = jax.nn.silu(cvec) @ w + b
    return jnp.split(m[..., None, :], 6, axis=-1)


def modulate(x, g, shift, scale):
    return rms_norm(x, g) * (1 + scale) + shift


def axial_rope(n_tokens, head_dim):
    t = jnp.arange(n_tokens)
    quarter = head_dim // 4
    inv = ROPE_THETA ** (-jnp.arange(quarter, dtype=jnp.float32) / quarter)
    rows = (t // GRID_W).astype(jnp.float32)[:, None] * inv
    cols = (t % GRID_W).astype(jnp.float32)[:, None] * inv
    ang = jnp.concatenate([rows, cols], axis=-1)
    return jnp.cos(ang), jnp.sin(ang)


def apply_rope(x, cos, sin):
    xf = x.astype(jnp.float32)
    x1, x2 = jnp.split(xf, 2, axis=-1)
    cs, sn = cos[:, None, :], sin[:, None, :]
    return jnp.concatenate([x1 * cs - x2 * sn, x1 * sn + x2 * cs], axis=-1).astype(x.dtype)


def gqa_attend(q, k, v):
    bn, lq, hq, dh = q.shape
    hkv = k.shape[2]
    qg = q.reshape(bn, lq, hkv, hq // hkv, dh)
    s = jnp.einsum('bqhgd,bkhd->bhgqk', qg, k).astype(jnp.float32) * (dh ** -0.5)
    p = jax.nn.softmax(s, axis=-1).astype(v.dtype)
    return jnp.einsum('bhgqk,bkhd->bqhgd', p, v).reshape(bn, lq, hq * dh)


def gqa_blocked(q, k, v, k_ctx, v_ctx):
    bn, s_len, hq, dh = q.shape
    keys = jnp.concatenate([k, k_ctx], axis=1)
    vals = jnp.concatenate([v, v_ctx], axis=1)
    qb = q.reshape(bn, s_len // C_QBLOCK, C_QBLOCK, hq, dh).transpose(1, 0, 2, 3, 4)
    ob = lax.map(lambda qi: gqa_attend(qi, keys, vals), qb)
    return ob.transpose(1, 0, 2, 3).reshape(bn, s_len, hq * dh)


def chunk_gmlp(u, v, w_s, b_s, g_v):
    bn, length, _ = u.shape
    u = jax.nn.gelu(u)
    v = layer_norm(jax.nn.gelu(v), g_v)
    v = v.reshape(bn, length // A_CHUNK, A_CHUNK, A_GROUPS, A_GDIM)
    s = jnp.einsum('gij,bnjgc->bnigc', w_s, v) + b_s.T[None, None, :, :, None]
    return u * s.reshape(bn, length, A_WIDTH)


def neighbourhood_attention(q, k, v, k_ctx, v_ctx, rpb):
    bn, s_len, nh, dh = q.shape
    rows = s_len // GRID_W
    kr = min(NA_ROWS, rows)
    ncb = GRID_W // NA_QCB
    qcol = np.arange(GRID_W).reshape(ncb, NA_QCB)
    kstart = np.clip(np.arange(ncb) * NA_QCB - NA_COLS // 2, 0, GRID_W - NA_KCB)
    kcol = kstart[:, None] + np.arange(NA_KCB)
    c0 = np.clip(qcol - NA_COLS // 2, 0, GRID_W - NA_COLS)
    kc3 = kcol[:, None, :]
    in_win = (kc3 >= c0[..., None]) & (kc3 < c0[..., None] + NA_COLS)
    dc_idx = np.clip(kc3 - qcol[..., None], -(NA_COLS - 1), NA_COLS - 1) + NA_COLS - 1
    mask = in_win[:, :, None, :]
    qg = q.reshape(bn, rows, ncb, NA_QCB, nh, dh)
    kgc = k.reshape(bn, rows, GRID_W, nh, dh)[:, :, kcol]
    vgc = v.reshape(bn, rows, GRID_W, nh, dh)[:, :, kcol]
    scale = dh ** -0.5
    s_ctx_all = None
    n_loc = kr * NA_KCB

    def row_block(r):
        r0 = jnp.clip(r - kr // 2, 0, rows - kr)
        q_r = lax.dynamic_index_in_dim(qg, r, axis=1, keepdims=False)
        k_r = lax.dynamic_slice_in_dim(kgc, r0, kr, axis=1)
        v_r = lax.dynamic_slice_in_dim(vgc, r0, kr, axis=1)
        s_loc = jnp.einsum('bnqhd,brnkhd->bhnqrk', q_r, k_r).astype(jnp.float32) * scale
        dr_idx = r0 + jnp.arange(kr) - r + NA_ROWS - 1
        bias = rpb[:, dr_idx][:, :, dc_idx].transpose(0, 2, 3, 1, 4)
        s_loc = jnp.where(mask, s_loc + bias.astype(jnp.float32)[None], NEG_INF)
        s_ctx = jnp.einsum('bnqhd,bchd->bhnqc', q_r, k_ctx).astype(jnp.float32) * scale
        s = jnp.concatenate([s_loc.reshape(bn, nh, ncb, NA_QCB, n_loc), s_ctx], axis=-1)
        p = jax.nn.softmax(s, axis=-1).astype(v.dtype)
        p_loc = p[..., :n_loc].reshape(bn, nh, ncb, NA_QCB, kr, NA_KCB)
        o = jnp.einsum('bhnqrk,brnkhd->bnqhd', p_loc, v_r)
        return o + jnp.einsum('bhnqc,bchd->bnqhd', p[..., n_loc:], v_ctx)

    o = lax.map(row_block, jnp.arange(rows))
    return o.transpose(1, 0, 2, 3, 4, 5).reshape(bn, s_len, nh * dh)


def even_mixer(xm, cm, w_in, w_s, b_s, g_v, rpb, w_out, with_ctx):
    splits = [A_WIDTH, 2 * A_WIDTH, 2 * A_WIDTH + B_WIDTH, 2 * A_WIDTH + 2 * B_WIDTH]

    def proj(t):
        u, va, q, k, vb = jnp.split(t @ w_in, splits, axis=-1)
        hd = lambda z: z.reshape(*z.shape[:-1], B_HEADS, B_HDIM)
        return u, va, hd(q), hd(k), hd(vb)

    u, va, q, k, vb = proj(xm)
    uc, vac, qc, kc, vbc = proj(cm)
    y_a = chunk_gmlp(u, va, w_s, b_s, g_v)
    y_b = neighbourhood_attention(q, k, vb, kc, vbc, rpb)
    y = jnp.concatenate([y_a, y_b], axis=-1) @ w_out
    yc = None
    if with_ctx:
        yc_a = chunk_gmlp(uc, vac, w_s, b_s, g_v)
        yc_b = gqa_attend(qc, kc, vbc)
        yc = jnp.concatenate([yc_a, yc_b], axis=-1) @ w_out
    return y, yc


def odd_mixer(xm, cm, w_qkv, q_g, k_g, w_out, cos, sin, with_ctx):
    splits = [C_HEADS * C_HDIM, (C_HEADS + C_KV_HEADS) * C_HDIM]

    def proj(t):
        q, k, v = jnp.split(t @ w_qkv, splits, axis=-1)
        q = rms_norm(q.reshape(*q.shape[:-1], C_HEADS, C_HDIM), q_g)
        k = rms_norm(k.reshape(*k.shape[:-1], C_KV_HEADS, C_HDIM), k_g)
        return q, k, v.reshape(*v.shape[:-1], C_KV_HEADS, C_HDIM)

    q, k, v = proj(xm)
    qc, kc, vc = proj(cm)
    q = apply_rope(q, cos, sin)
    k = apply_rope(k, cos, sin)
    y = gqa_blocked(q, k, v, kc, vc) @ w_out
    yc = gqa_attend(qc, kc, vc) @ w_out if with_ctx else None
    return y, yc


def conv_ffn(x, w_up, conv_w, conv_b, w_down):
    h = x @ w_up
    h = lax.conv_general_dilated(h, conv_w[:, None, :], window_strides=(1,),
                                 padding=((CONV_W // 2, CONV_W // 2),),
                                 dimension_numbers=('NWC', 'WIO', 'NWC'),
                                 feature_group_count=h.shape[-1]) + conv_b
    a, g = jnp.split(h, 2, axis=-1)
    return (jax.nn.silu(g) * a) @ w_down


def setup_inputs(seed: int = 0) -> dict:
    key = jax.random.key(seed)
    ks = jax.random.split(key, 24)
    nrm = lambda k, shape, s: jax.random.normal(k, shape, jnp.float32) * s
    return {
        "x": nrm(ks[0], (BATCH, SEQ, D_MODEL), 1.0),
        "c": nrm(ks[1], (BATCH, D_MODEL), 1.0),
        "ctx": nrm(ks[2], (BATCH, CTX_LEN, D_MODEL), 1.0),
        "c_ctx": nrm(ks[3], (D_MODEL,), 1.0),
        "w_ada": nrm(ks[4], (DEPTH, D_MODEL, 6 * D_MODEL), 0.5 * D_MODEL ** -0.5),
        "b_ada": nrm(ks[5], (DEPTH, 6 * D_MODEL), 0.02),
        "norm_g": 1.0 + nrm(ks[6], (DEPTH, 4, D_MODEL), 0.05),
        "w_in_ab": nrm(ks[7], (N_EVEN, D_MODEL, IN_AB), D_MODEL ** -0.5),
        "a_w_s": nrm(ks[8], (N_EVEN, A_GROUPS, A_CHUNK, A_CHUNK), A_CHUNK ** -0.5),
        "a_b_s": 1.0 + nrm(ks[9], (N_EVEN, A_GROUPS, A_CHUNK), 0.1),
        "a_v_g": 1.0 + nrm(ks[10], (N_EVEN, A_WIDTH), 0.05),
        "b_rpb": nrm(ks[11], (N_EVEN, B_HEADS, 2 * NA_ROWS - 1, 2 * NA_COLS - 1), 0.5),
        "w_out_ab": nrm(ks[12], (N_EVEN, MIX_WIDTH, D_MODEL), MIX_WIDTH ** -0.5),
        "w_qkv_c": nrm(ks[13], (N_ODD, D_MODEL, IN_C), D_MODEL ** -0.5),
        "c_q_g": 1.0 + nrm(ks[14], (N_ODD, C_HDIM), 0.05),
        "c_k_g": 1.0 + nrm(ks[15], (N_ODD, C_HDIM), 0.05),
        "w_out_c": nrm(ks[16], (N_ODD, C_HEADS * C_HDIM, D_MODEL), (C_HEADS * C_HDIM) ** -0.5),
        "w_up": nrm(ks[17], (DEPTH, D_MODEL, 2 * D_FF), D_MODEL ** -0.5),
        "conv_w": nrm(ks[18], (DEPTH, CONV_W, 2 * D_FF), CONV_W ** -0.5),
        "conv_b": nrm(ks[19], (DEPTH, 2 * D_FF), 0.02),
        "w_down": nrm(ks[20], (DEPTH, D_FF, D_MODEL), D_FF ** -0.5),
    }


def reference(x, c, ctx, c_ctx, w_ada, b_ada, norm_g, w_in_ab, a_w_s, a_b_s, a_v_g, b_rpb,
              w_out_ab, w_qkv_c, c_q_g, c_k_g, w_out_c, w_up, conv_w, conv_b, w_down):
    cos, sin = axial_rope(x.shape[1], C_HDIM)
    h, hc = x, ctx
    for layer in range(DEPTH):
        with_ctx = layer < DEPTH - 1
        sh_m, sc_m, gt_m, sh_f, sc_f, gt_f = ada_chunks(c, w_ada[layer], b_ada[layer])
        csh_m, csc_m, cgt_m, csh_f, csc_f, cgt_f = ada_chunks(c_ctx, w_ada[layer], b_ada[layer])
        g_pre_m, g_post_m, g_pre_f, g_post_f = norm_g[layer]
        xm = modulate(h, g_pre_m, sh_m, sc_m)
        cm = modulate(hc, g_pre_m, csh_m, csc_m)
        if layer % 2 == 0:
            e = layer // 2
            y, yc = even_mixer(xm, cm, w_in_ab[e], a_w_s[e], a_b_s[e], a_v_g[e], b_rpb[e],
                               w_out_ab[e], with_ctx)
        else:
            o = layer // 2
            y, yc = odd_mixer(xm, cm, w_qkv_c[o], c_q_g[o], c_k_g[o], w_out_c[o], cos, sin, with_ctx)
        h = h + gt_m * rms_norm(y, g_post_m)
        f = conv_ffn(modulate(h, g_pre_f, sh_f, sc_f), w_up[layer], conv_w[layer], conv_b[layer], w_down[layer])
        h = h + gt_f * rms_norm(f, g_post_f)
        if with_ctx:
            hc = hc + cgt_m * rms_norm(yc, g_post_m)
            fc = conv_ffn(modulate(hc, g_pre_f, csh_f, csc_f), w_up[layer], conv_w[layer], conv_b[layer], w_down[layer])
            hc = hc + cgt_f * rms_norm(fc, g_post_f)
    return h
```

```python
import functools

import numpy as np
import jax
import jax.numpy as jnp
from jax import lax
from jax.experimental import pallas as pl
from jax.experimental.pallas import tpu as pltpu

F32 = jnp.float32
BF16 = jnp.bfloat16

EPS = 1e-6
NEG_INF = -1e30
GRID_W = 64
ROPE_THETA = 10000.0
A_CHUNK = 128
A_GROUPS = 4
NA_ROWS = 8
NA_COLS = 16
HDIM = 64
LANES = 128
NA_QROWS = 4
NA_KROWS = 12
FFN_HALO = 16
FFN_CW = 256
V7X_VMEM_BYTES = 64 * 1024 * 1024


def _cparams(sem, vmem_mb):
    assert vmem_mb * 1024 * 1024 < V7X_VMEM_BYTES
    return pltpu.CompilerParams(dimension_semantics=sem, vmem_limit_bytes=vmem_mb * 1024 * 1024)


def _tile(n, pref):
    return pref if n % pref == 0 else n


def _rms(x, g):
    return x * lax.rsqrt(jnp.mean(x * x, axis=-1, keepdims=True) + EPS) * g


def _modulate(x, g, shift, scale):
    return _rms(x, g) * (1.0 + scale) + shift


def _gelu(x):
    return 0.5 * x * (1.0 + jnp.tanh(0.7978845608028654 * (x + 0.044715 * (x * x * x))))


def _silu(x):
    return x * jax.nn.sigmoid(x)


def _dot(a, b):
    return jnp.dot(a, b, preferred_element_type=F32)


def _dot_t(a, b):
    return lax.dot_general(a, b, (((1,), (1,)), ((), ())), preferred_element_type=F32)


def _ada_body(c_ref, w_ref, b_ref, o_ref):
    s = _silu(c_ref[...])
    o_ref[0] = jnp.dot(s, w_ref[0], preferred_element_type=F32,
                       precision=lax.Precision.HIGHEST) + b_ref[0]


def _ada(cv, w_ada, b_ada):
    depth, d, n = w_ada.shape
    rows = cv.shape[0]
    tn = _tile(n, 1536)
    return pl.pallas_call(
        _ada_body,
        grid=(depth, n // tn),
        in_specs=[pl.BlockSpec((rows, d), lambda l, j: (0, 0)),
                  pl.BlockSpec((1, d, tn), lambda l, j: (l, 0, j)),
                  pl.BlockSpec((1, 1, tn), lambda l, j: (l, 0, j))],
        out_specs=pl.BlockSpec((1, rows, tn), lambda l, j: (l, 0, j)),
        out_shape=jax.ShapeDtypeStruct((depth, rows, n), F32),
        compiler_params=_cparams(("parallel", "parallel"), 40),
        name="ada",
    )(cv, w_ada, b_ada.reshape(depth, 1, n))


def _inproj_ab_body(h_ref, sh_ref, sc_ref, g_ref, w_ref, ws_ref, bs_ref, gv_ref,
                    ya_ref, q_ref, k_ref, v_ref, *, tm, aw):
    xm = _modulate(h_ref[0], g_ref[...], sh_ref[0], sc_ref[0]).astype(BF16)
    u = _gelu(_dot(xm, w_ref[:, 0:aw]))
    va = _gelu(_dot(xm, w_ref[:, aw:2 * aw]))
    mu = jnp.mean(va, axis=-1, keepdims=True)
    d = va - mu
    var = jnp.mean(d * d, axis=-1, keepdims=True)
    vn = (d * lax.rsqrt(var + EPS) * gv_ref[...]).astype(BF16)
    gd = aw // A_GROUPS
    for n in range(tm // A_CHUNK):
        r0, r1 = n * A_CHUNK, (n + 1) * A_CHUNK
        for g in range(A_GROUPS):
            c0, c1 = g * gd, (g + 1) * gd
            s = _dot(ws_ref[g], vn[r0:r1, c0:c1]) + bs_ref[g]
            ya_ref[0, r0:r1, c0:c1] = (u[r0:r1, c0:c1] * s).astype(BF16)
    bw = q_ref.shape[-1]
    o = 2 * aw
    q_ref[0] = (_dot(xm, w_ref[:, o:o + bw]) * (HDIM ** -0.5)).astype(BF16)
    k_ref[0] = _dot(xm, w_ref[:, o + bw:o + 2 * bw]).astype(BF16)
    v_ref[0] = _dot(xm, w_ref[:, o + 2 * bw:o + 3 * bw]).astype(BF16)


def _inproj_ab(h, sh, sc, g_pre, w_in, w_s, b_s_b, g_v, aw, bw):
    b, l, d = h.shape
    tm = _tile(l, 256)
    n_in = w_in.shape[1]
    row = lambda bi, i: (bi, i, 0)
    vec = lambda bi, i: (bi, 0, 0)
    c2 = lambda bi, i: (0, 0)
    c3 = lambda bi, i: (0, 0, 0)
    body = functools.partial(_inproj_ab_body, tm=tm, aw=aw)
    return pl.pallas_call(
        body,
        grid=(b, l // tm),
        in_specs=[pl.BlockSpec((1, tm, d), row),
                  pl.BlockSpec((1, 1, d), vec), pl.BlockSpec((1, 1, d), vec),
                  pl.BlockSpec((1, d), c2),
                  pl.BlockSpec((d, n_in), c2),
                  pl.BlockSpec(w_s.shape, c3), pl.BlockSpec(b_s_b.shape, c3),
                  pl.BlockSpec((1, aw), c2)],
        out_specs=[pl.BlockSpec((1, tm, aw), row)] + [pl.BlockSpec((1, tm, bw), row)] * 3,
        out_shape=[jax.ShapeDtypeStruct((b, l, aw), BF16)] + [jax.ShapeDtypeStruct((b, l, bw), BF16)] * 3,
        compiler_params=_cparams(("parallel", "parallel"), 48),
        name="inproj_ab",
    )(h, sh, sc, g_pre, w_in, w_s, b_s_b, g_v)


def _softmax_parts(parts):
    m = parts[0].max(axis=-1, keepdims=True)
    for p in parts[1:]:
        m = jnp.maximum(m, p.max(axis=-1, keepdims=True))
    es = [jnp.exp(p - m) for p in parts]
    tot = es[0].sum(axis=-1, keepdims=True)
    for e in es[1:]:
        tot = tot + e.sum(axis=-1, keepdims=True)
    return es, tot


def _na_body(q_ref, k_ref, v_ref, kc_ref, vc_ref, t_ref, o_ref, *, rows):
    rb = pl.program_id(2)
    tq = NA_QROWS * GRID_W
    nk = NA_KROWS * GRID_W
    start = jnp.clip(NA_QROWS * rb - NA_ROWS // 2, 0, rows - NA_KROWS)
    koff = pl.multiple_of(start * GRID_W, NA_QROWS * GRID_W)
    kw = k_ref[0, pl.ds(koff, nk), :]
    vw = v_ref[0, pl.ds(koff, nk), :]
    kc = kc_ref[0]
    vc = vc_ref[0]
    q = q_ref[0]
    lo_q = lax.broadcasted_iota(jnp.int32, (tq, LANES), 1) < HDIM
    lo_b = lax.broadcasted_iota(jnp.int32, (GRID_W, LANES), 1) < GRID_W
    idx = []
    for ri in range(NA_QROWS):
        r = NA_QROWS * rb + ri
        r0 = jnp.clip(r - NA_ROWS // 2, 0, rows - NA_ROWS)
        row = []
        for j in range(NA_KROWS):
            kr = start + j
            valid = jnp.logical_and(kr >= r0, kr < r0 + NA_ROWS)
            row.append(jnp.where(valid, kr - r + NA_ROWS - 1, 2 * NA_ROWS - 1))
        idx.append(row)
    outs = []
    for a in range(2):
        qa = jnp.where(lo_q if a == 0 else jnp.logical_not(lo_q), q, jnp.zeros_like(q))
        s_loc = _dot_t(qa, kw)
        s_ctx = _dot_t(qa, kc)
        bias = jnp.concatenate([
            jnp.concatenate([
                jnp.where(lo_b, t_ref[a, idx[ri][2 * m]], t_ref[a, idx[ri][2 * m + 1]])
                for m in range(NA_KROWS // 2)], axis=1)
            for ri in range(NA_QROWS)], axis=0)
        (e_loc, e_ctx), tot = _softmax_parts([s_loc + bias, s_ctx])
        o = _dot(e_loc.astype(BF16), vw) + _dot(e_ctx.astype(BF16), vc)
        outs.append(o / tot)
    o_ref[0] = jnp.where(lo_q, outs[0], outs[1]).astype(BF16)


def _na_attention(q, k, v, kc, vc, tdup):
    b, s, w = q.shape
    lc = kc.shape[1]
    rows = s // GRID_W
    tq = NA_QROWS * GRID_W
    pairs = w // LANES
    body = functools.partial(_na_body, rows=rows)
    return pl.pallas_call(
        body,
        grid=(b, pairs, rows // NA_QROWS),
        in_specs=[pl.BlockSpec((1, tq, LANES), lambda bi, p, r: (bi, r, p)),
                  pl.BlockSpec((1, s, LANES), lambda bi, p, r: (bi, 0, p)),
                  pl.BlockSpec((1, s, LANES), lambda bi, p, r: (bi, 0, p)),
                  pl.BlockSpec((1, lc, LANES), lambda bi, p, r: (bi, 0, p)),
                  pl.BlockSpec((1, lc, LANES), lambda bi, p, r: (bi, 0, p)),
                  pl.BlockSpec((2,) + tdup.shape[1:], lambda bi, p, r: (p, 0, 0, 0))],
        out_specs=pl.BlockSpec((1, tq, LANES), lambda bi, p, r: (bi, r, p)),
        out_shape=jax.ShapeDtypeStruct((b, s, w), BF16),
        compiler_params=_cparams(("parallel", "parallel", "arbitrary"), 40),
        name="na_attention",
    )(q, k, v, kc, vc, tdup)


def _ctx_attn_body(q_ref, k_ref, v_ref, o_ref):
    q = q_ref[0]
    k = k_ref[0]
    v = v_ref[0]
    lo = lax.broadcasted_iota(jnp.int32, q.shape, 1) < HDIM
    outs = []
    for a in range(2):
        qa = jnp.where(lo if a == 0 else jnp.logical_not(lo), q, jnp.zeros_like(q))
        (e,), tot = _softmax_parts([_dot_t(qa, k)])
        outs.append(_dot(e.astype(BF16), v) / tot)
    o_ref[0] = jnp.where(lo, outs[0], outs[1]).astype(BF16)


def _ctx_attention(q, k, v):
    b, l, w = q.shape
    spec = pl.BlockSpec((1, l, LANES), lambda bi, p: (bi, 0, p))
    return pl.pallas_call(
        _ctx_attn_body,
        grid=(b, w // LANES),
        in_specs=[spec, spec, spec],
        out_specs=spec,
        out_shape=jax.ShapeDtypeStruct((b, l, w), BF16),
        compiler_params=_cparams(("parallel", "parallel"), 32),
        name="ctx_attention",
    )(q, k, v)


def _outproj_body(*refs, n_parts):
    y_refs = refs[:n_parts]
    w_refs = refs[n_parts:2 * n_parts]
    h_ref, gt_ref, g_ref, o_ref = refs[2 * n_parts:]
    y = _dot(y_refs[0][0], w_refs[0][...])
    for yr, wr in zip(y_refs[1:], w_refs[1:]):
        y = y + _dot(yr[0], wr[...])
    o_ref[0] = h_ref[0] + gt_ref[0] * _rms(y, g_ref[...])


def _outproj(ys, ws, h, gt, g_post):
    b, l, d = h.shape
    tm = _tile(l, 512)
    row = lambda bi, i: (bi, i, 0)
    vec = lambda bi, i: (bi, 0, 0)
    c2 = lambda bi, i: (0, 0)
    body = functools.partial(_outproj_body, n_parts=len(ys))
    return pl.pallas_call(
        body,
        grid=(b, l // tm),
        in_specs=([pl.BlockSpec((1, tm, y.shape[-1]), row) for y in ys]
                  + [pl.BlockSpec(w.shape, c2) for w in ws]
                  + [pl.BlockSpec((1, tm, d), row), pl.BlockSpec((1, 1, d), vec), pl.BlockSpec((1, d), c2)]),
        out_specs=pl.BlockSpec((1, tm, d), row),
        out_shape=jax.ShapeDtypeStruct((b, l, d), F32),
        compiler_params=_cparams(("parallel", "parallel"), 40),
        name="outproj",
    )(*ys, *ws, h, gt, g_post)


def _ffn_body(xp_ref, x_ref, xn_ref, sh_ref, sc_ref, gt_ref, gpre_ref, gpost_ref,
              wa_ref, wg_ref, cwa_ref, cwg_ref, cba_ref, cbg_ref, wd_ref,
              o_ref, xs_ref, acc_ref, *, tm, n_tiles, n_chunks):
    i = pl.program_id(1)
    g = gpre_ref[...]
    sh = sh_ref[0]
    sc = sc_ref[0]
    x = x_ref[0]
    hl = FFN_HALO
    has_prev = (i > 0).astype(F32)
    has_next = (i < n_tiles - 1).astype(F32)
    xs_ref[0:hl] = (_modulate(xp_ref[0], g, sh, sc) * has_prev).astype(BF16)
    xs_ref[hl:hl + tm] = _modulate(x, g, sh, sc).astype(BF16)
    xs_ref[hl + tm:hl + tm + hl] = (_modulate(xn_ref[0], g, sh, sc) * has_next).astype(BF16)
    acc_ref[...] = jnp.zeros_like(acc_ref)

    def conv(hh, cw, cb):
        return (hh[hl - 1:hl - 1 + tm] * cw[0:1] + hh[hl:hl + tm] * cw[1:2]
                + hh[hl + 1:hl + 1 + tm] * cw[2:3] + cb)

    def chunk(c, carry):
        xs = xs_ref[...]
        a = conv(_dot(xs, wa_ref[c]), cwa_ref[c], cba_ref[c])
        gg = conv(_dot(xs, wg_ref[c]), cwg_ref[c], cbg_ref[c])
        act = (_silu(gg) * a).astype(BF16)
        acc_ref[...] += _dot(act, wd_ref[c])
        return carry

    lax.fori_loop(0, n_chunks, chunk, 0)
    o_ref[0] = x + gt_ref[0] * _rms(acc_ref[...], gpost_ref[...])


def _ffn(h, sh, sc, gt, g_pre, g_post, wts):
    wa, wg, cwa, cwg, cba, cbg, wd = wts
    b, l, d = h.shape
    tm = _tile(l, 512)
    n_tiles = l // tm
    n_chunks = wa.shape[0]
    hb = tm // FFN_HALO
    last_hb = l // FFN_HALO - 1
    row = lambda bi, i: (bi, i, 0)
    vec = lambda bi, i: (bi, 0, 0)
    c2 = lambda bi, i: (0, 0)
    c3 = lambda bi, i: (0, 0, 0)
    whole = lambda a: pl.BlockSpec(a.shape, c3, pipeline_mode=pl.Buffered(1))
    body = functools.partial(_ffn_body, tm=tm, n_tiles=n_tiles, n_chunks=n_chunks)
    return pl.pallas_call(
        body,
        grid=(b, n_tiles),
        in_specs=[pl.BlockSpec((1, FFN_HALO, d), lambda bi, i: (bi, jnp.maximum(i * hb - 1, 0), 0)),
                  pl.BlockSpec((1, tm, d), row),
                  pl.BlockSpec((1, FFN_HALO, d), lambda bi, i: (bi, jnp.minimum((i + 1) * hb, last_hb), 0)),
                  pl.BlockSpec((1, 1, d), vec), pl.BlockSpec((1, 1, d), vec), pl.BlockSpec((1, 1, d), vec),
                  pl.BlockSpec((1, d), c2), pl.BlockSpec((1, d), c2),
                  whole(wa), whole(wg), whole(cwa), whole(cwg), whole(cba), whole(cbg), whole(wd)],
        out_specs=pl.BlockSpec((1, tm, d), row),
        out_shape=jax.ShapeDtypeStruct((b, l, d), F32),
        scratch_shapes=[pltpu.VMEM((tm + 2 * FFN_HALO, d), BF16), pltpu.VMEM((tm, d), F32)],
        compiler_params=_cparams(("parallel", "parallel"), 56),
        name="conv_ffn",
    )(h, h, h, sh, sc, gt, g_pre, g_post, wa, wg, cwa, cwg, cba, cbg, wd)


def _ffn_weights(w_up, conv_w, conv_b, w_down):
    d, two_ff = w_up.shape
    ff = two_ff // 2
    nch = ff // FFN_CW
    cols = lambda w: w.reshape(w.shape[0], nch, FFN_CW).transpose(1, 0, 2)
    return (cols(w_up[:, :ff].astype(BF16)), cols(w_up[:, ff:].astype(BF16)),
            cols(conv_w[:, :ff]), cols(conv_w[:, ff:]),
            cols(conv_b[None, :ff]), cols(conv_b[None, ff:]),
            w_down.astype(BF16).reshape(nch, FFN_CW, d))


def _inproj_c_body(h_ref, sh_ref, sc_ref, g_ref, wq_ref, wk_ref, wv_ref, gq_ref, gk_ref, bd_ref,
                   cos_ref, sin_ref, *out_refs, n_qtiles):
    xm = _modulate(h_ref[0], g_ref[...], sh_ref[0], sc_ref[0]).astype(BF16)
    cs = cos_ref[...]
    sn = sin_ref[...]
    bd = bd_ref[...]

    def norm_rope(w_ref, o_ref, c0, gains):
        t1 = _dot(xm, w_ref[:, c0:c0 + LANES])
        t2 = _dot(xm, w_ref[:, c0 + LANES:c0 + 2 * LANES])
        ss = _dot((t1 * t1 + t2 * t2).astype(BF16), bd)
        r = lax.rsqrt(ss * (1.0 / HDIM) + EPS)
        a1 = t1 * r * gains[0:1]
        a2 = t2 * r * gains[1:2]
        o_ref[0, :, c0:c0 + LANES] = (a1 * cs - a2 * sn).astype(BF16)
        o_ref[0, :, c0 + LANES:c0 + 2 * LANES] = (a1 * sn + a2 * cs).astype(BF16)

    if n_qtiles:
        q_ref, k_ref, v_ref = out_refs
        for jj in range(n_qtiles):
            norm_rope(wq_ref, q_ref, 2 * LANES * jj, gq_ref[...])
    else:
        k_ref, v_ref = out_refs
    norm_rope(wk_ref, k_ref, 0, gk_ref[...])
    v_ref[0] = _dot(xm, wv_ref[...]).astype(BF16)


def _inproj_c(h, sh, sc, g_pre, wq, wk, wv, gq, gk, bd, cos, sin, with_q):
    b, l, d = h.shape
    tm = _tile(l, 256)
    row = lambda bi, i: (bi, i, 0)
    vec = lambda bi, i: (bi, 0, 0)
    c2 = lambda bi, i: (0, 0)
    nq, nkv = wq.shape[1], wk.shape[1]
    n_qtiles = nq // (2 * LANES) if with_q else 0
    body = functools.partial(_inproj_c_body, n_qtiles=n_qtiles)
    out_specs = [pl.BlockSpec((1, tm, nkv), row)] * 2
    out_shape = [jax.ShapeDtypeStruct((b, l, nkv), BF16)] * 2
    if with_q:
        out_specs = [pl.BlockSpec((1, tm, nq), row)] + out_specs
        out_shape = [jax.ShapeDtypeStruct((b, l, nq), BF16)] + out_shape
    return pl.pallas_call(
        body,
        grid=(b, l // tm),
        in_specs=[pl.BlockSpec((1, tm, d), row),
                  pl.BlockSpec((1, 1, d), vec), pl.BlockSpec((1, 1, d), vec),
                  pl.BlockSpec((1, d), c2),
                  pl.BlockSpec(wq.shape, c2), pl.BlockSpec(wk.shape, c2), pl.BlockSpec(wv.shape, c2),
                  pl.BlockSpec(gq.shape, c2), pl.BlockSpec(gk.shape, c2), pl.BlockSpec(bd.shape, c2),
                  pl.BlockSpec((tm, LANES), lambda bi, i: (i, 0)),
                  pl.BlockSpec((tm, LANES), lambda bi, i: (i, 0))],
        out_specs=out_specs,
        out_shape=out_shape,
        compiler_params=_cparams(("parallel", "parallel"), 40),
        name="inproj_c",
    )(h, sh, sc, g_pre, wq, wk, wv, gq, gk, bd, cos, sin)


def _gqa_body(q_ref, k_ref, v_ref, kc_ref, vc_ref, o_ref, *, tq):
    q = q_ref[0]
    seg = (lax.broadcasted_iota(jnp.int32, q.shape, 1) % LANES) // (HDIM // 2)
    zero = jnp.zeros_like(q)
    lhs = jnp.concatenate([jnp.where(seg == s, q, zero) for s in range(4)], axis=0)
    (e_lat, e_ctx), tot = _softmax_parts([_dot_t(lhs, k_ref[0]), _dot_t(lhs, kc_ref[0])])
    p_lat = e_lat.astype(BF16)
    p_ctx = e_ctx.astype(BF16)
    inv = 1.0 / tot
    lo = lax.broadcasted_iota(jnp.int32, (tq, LANES), 1) < HDIM
    outs = []
    for s in range(4):
        c0 = (s // 2) * LANES
        rs = slice(s * tq, (s + 1) * tq)
        o = _dot(p_lat[rs], v_ref[0, :, c0:c0 + LANES]) + _dot(p_ctx[rs], vc_ref[0, :, c0:c0 + LANES])
        outs.append(o * inv[rs])
    o_ref[0] = jnp.concatenate([jnp.where(lo, outs[0], outs[1]),
                                jnp.where(lo, outs[2], outs[3])], axis=1).astype(BF16)


def _gqa_attention(q, k, v, kc, vc):
    b, s, nq = q.shape
    lc = kc.shape[1]
    nkv = k.shape[-1]
    tq = 128
    body = functools.partial(_gqa_body, tq=tq)
    kv_spec = lambda n: pl.BlockSpec((1, n, nkv), lambda bi, i, j: (bi, 0, 0))
    return pl.pallas_call(
        body,
        grid=(b, s // tq, nq // (2 * LANES)),
        in_specs=[pl.BlockSpec((1, tq, 2 * LANES), lambda bi, i, j: (bi, i, j)),
                  kv_spec(s), kv_spec(s), kv_spec(lc), kv_spec(lc)],
        out_specs=pl.BlockSpec((1, tq, 2 * LANES), lambda bi, i, j: (bi, i, j)),
        out_shape=jax.ShapeDtypeStruct((b, s, nq), BF16),
        compiler_params=_cparams(("parallel", "arbitrary", "arbitrary"), 48),
        name="gqa_attention",
    )(q, k, v, kc, vc)


def _na_bias_table(rpb):
    qc = np.arange(GRID_W)[:, None]
    kc = np.arange(GRID_W)[None, :]
    c0 = np.clip(qc - NA_COLS // 2, 0, GRID_W - NA_COLS)
    in_win = (kc >= c0) & (kc < c0 + NA_COLS)
    dc = np.clip(kc - qc, -(NA_COLS - 1), NA_COLS - 1) + NA_COLS - 1
    t = jnp.where(in_win, rpb[:, :, dc], NEG_INF)
    t = jnp.concatenate([t, jnp.full_like(t[:, :1], NEG_INF)], axis=1)
    return jnp.concatenate([t, t], axis=-1)


def _gqa_layout(n_heads, n_kv):
    half = HDIM // 2
    q_cols = [(n_kv * s + jj) * HDIM + p * half + i
              for jj in range(n_heads // n_kv) for p in range(2) for s in range(n_kv) for i in range(half)]
    k_cols = [s * HDIM + p * half + i for p in range(2) for s in range(n_kv) for i in range(half)]
    y_rows = [(n_kv * s + jj) * HDIM + c
              for jj in range(n_heads // n_kv) for s in range(n_kv) for c in range(HDIM)]
    return np.array(q_cols), np.array(k_cols), np.array(y_rows)


def _rope_tables(n_tokens):
    t = jnp.arange(n_tokens)
    quarter = HDIM // 4
    inv = ROPE_THETA ** (-jnp.arange(quarter, dtype=F32) / quarter)
    rows = (t // GRID_W).astype(F32)[:, None] * inv
    cols = (t % GRID_W).astype(F32)[:, None] * inv
    ang = jnp.tile(jnp.concatenate([rows, cols], axis=-1), (1, LANES // (HDIM // 2)))
    return jnp.cos(ang), jnp.sin(ang)


def kernel(x, c, ctx, c_ctx, w_ada, b_ada, norm_g, w_in_ab, a_w_s, a_b_s, a_v_g, b_rpb, w_out_ab,
           w_qkv_c, c_q_g, c_k_g, w_out_c, w_up, conv_w, conv_b, w_down):
    bsz, seq, d = x.shape
    lc = ctx.shape[1]
    depth = w_ada.shape[0]
    aw = a_v_g.shape[-1]
    n_heads_b = b_rpb.shape[1]
    bw = n_heads_b * HDIM
    n_kv = 4
    n_heads_c = w_out_c.shape[1] // HDIM
    assert seq % (NA_QROWS * GRID_W) == 0 and seq // GRID_W >= NA_KROWS
    assert aw // A_GROUPS == A_CHUNK and w_up.shape[-1] // 2 % FFN_CW == 0

    n_rows = 8 * ((bsz + 1 + 7) // 8)
    cv = jnp.zeros((n_rows, d), F32).at[:bsz].set(c).at[bsz].set(c_ctx)
    mods = _ada(cv, w_ada, b_ada)

    def mod_vectors(layer):
        m = mods[layer].reshape(n_rows, 6, d)
        lat = [m[:bsz, j][:, None, :] for j in range(6)]
        cx = [jnp.broadcast_to(m[bsz, j][None, None, :], (bsz, 1, d)) for j in range(6)]
        return lat, cx

    cos_l, sin_l = _rope_tables(seq)
    cos_c, sin_c = jnp.ones((lc, LANES), F32), jnp.zeros((lc, LANES), F32)

    h, hc = x, ctx
    for layer in range(depth):
        with_ctx = layer < depth - 1
        lat, cx = mod_vectors(layer)
        g_pre_m, g_post_m, g_pre_f, g_post_f = [norm_g[layer, j][None, :] for j in range(4)]
        if layer % 2 == 0:
            e = layer // 2
            w_in = w_in_ab[e].astype(BF16)
            w_s = a_w_s[e].astype(BF16)
            b_s_b = jnp.broadcast_to(a_b_s[e][:, :, None], a_w_s[e].shape[:2] + (aw // A_GROUPS,))
            g_v = a_v_g[e][None, :]
            w_out = w_out_ab[e].astype(BF16)
            ya, q, k, v = _inproj_ab(h, lat[0], lat[1], g_pre_m, w_in, w_s, b_s_b, g_v, aw, bw)
            yca, qc, kc, vc = _inproj_ab(hc, cx[0], cx[1], g_pre_m, w_in, w_s, b_s_b, g_v, aw, bw)
            yb = _na_attention(q, k, v, kc, vc, _na_bias_table(b_rpb[e]))
            h = _outproj([ya, yb], [w_out[:aw], w_out[aw:]], h, lat[2], g_post_m)
            if with_ctx:
                ycb = _ctx_attention(qc, kc, vc)
                hc = _outproj([yca, ycb], [w_out[:aw], w_out[aw:]], hc, cx[2], g_post_m)
        else:
            o = layer // 2
            q_cols, k_cols, y_rows = _gqa_layout(n_heads_c, n_kv)
            nq = n_heads_c * HDIM
            nkv = n_kv * HDIM
            wqkv = w_qkv_c[o]
            wq = wqkv[:, :nq][:, q_cols].astype(BF16)
            wk = wqkv[:, nq:nq + nkv][:, k_cols].astype(BF16)
            wv = wqkv[:, nq + nkv:].astype(BF16)
            half = HDIM // 2
            reps = LANES // half
            gq = jnp.stack([jnp.tile(c_q_g[o][:half], reps), jnp.tile(c_q_g[o][half:], reps)]) * (HDIM ** -0.5)
            gk = jnp.stack([jnp.tile(c_k_g[o][:half], reps), jnp.tile(c_k_g[o][half:], reps)])
            seg = np.arange(LANES) // half
            bd = jnp.asarray(seg[:, None] == seg[None, :], BF16)
            w_out = w_out_c[o][y_rows].astype(BF16)
            q, k, v = _inproj_c(h, lat[0], lat[1], g_pre_m, wq, wk, wv, gq, gk, bd, cos_l, sin_l, True)
            kc, vc = _inproj_c(hc, cx[0], cx[1], g_pre_m, wq, wk, wv, gq, gk, bd, cos_c, sin_c, False)
            y = _gqa_attention(q, k, v, kc, vc)
            h = _outproj([y], [w_out], h, lat[2], g_post_m)
            if with_ctx:
                raise NotImplementedError("context update after a grouped-query layer")
        wts = _ffn_weights(w_up[layer], conv_w[layer], conv_b[layer], w_down[layer])
        h = _ffn(h, lat[3], lat[4], lat[5], g_pre_f, g_post_f, wts)
        if with_ctx:
            hc = _ffn(hc, cx[3], cx[4], cx[5], g_pre_f, g_post_f, wts)
    return h
```

```python
import functools

import numpy as np
import jax
import jax.numpy as jnp
from jax import lax
from jax.experimental import pallas as pl
from jax.experimental.pallas import tpu as pltpu

F32 = jnp.float32
BF16 = jnp.bfloat16

EPS = 1e-6
NEG_INF = -1e30
GRID_W = 64
ROPE_THETA = 10000.0
LOG2E = 1.4426950408889634
A_CHUNK = 128
A_GROUPS = 4
NA_ROWS = 8
NA_COLS = 16
HDIM = 64
LANES = 128
NA_QROWS = 4
NA_KROWS = 12
SUBLANES = 8
FFN_CW = 256
V7X_VMEM_BYTES = 64 * 1024 * 1024


def _cparams(sem, vmem_mb):
    assert vmem_mb * 1024 * 1024 < V7X_VMEM_BYTES
    return pltpu.CompilerParams(dimension_semantics=sem, vmem_limit_bytes=vmem_mb * 1024 * 1024)


def _tile(n, pref):
    return pref if n % pref == 0 else n


def _rms(x, g):
    return x * lax.rsqrt(jnp.mean(x * x, axis=-1, keepdims=True) + EPS) * g


def _modulate(x, g, shift, scale):
    return _rms(x, g) * (1.0 + scale) + shift


def _gelu(x):
    return 0.5 * x * (1.0 + jnp.tanh(0.7978845608028654 * (x + 0.044715 * (x * x * x))))


def _silu(x):
    return x * jax.nn.sigmoid(x)


def _dot(a, b):
    return jnp.dot(a, b, preferred_element_type=F32)


def _dot_t(a, b):
    return lax.dot_general(a, b, (((1,), (1,)), ((), ())), preferred_element_type=F32)


def _ada_body(c_ref, w_ref, b_ref, o_ref):
    s = _silu(c_ref[...])
    o_ref[0] = jnp.dot(s, w_ref[0], preferred_element_type=F32,
                       precision=lax.Precision.HIGHEST) + b_ref[0]


def _ada(cv, w_ada, b_ada):
    depth, d, n = w_ada.shape
    rows = cv.shape[0]
    tn = _tile(n, 1536)
    return pl.pallas_call(
        _ada_body,
        grid=(depth, n // tn),
        in_specs=[pl.BlockSpec((rows, d), lambda l, j: (0, 0)),
                  pl.BlockSpec((1, d, tn), lambda l, j: (l, 0, j)),
                  pl.BlockSpec((1, 1, tn), lambda l, j: (l, 0, j))],
        out_specs=pl.BlockSpec((1, rows, tn), lambda l, j: (l, 0, j)),
        out_shape=jax.ShapeDtypeStruct((depth, rows, n), F32),
        compiler_params=_cparams(("parallel", "parallel"), 40),
        name="ada",
    )(cv, w_ada, b_ada.reshape(depth, 1, n))


def _inproj_ab_body(h_ref, sh_ref, sc_ref, g_ref, w_ref, ws_ref, bs_ref, gv_ref,
                    ya_ref, q_ref, k_ref, v_ref, *, tm, aw):
    xm = _modulate(h_ref[0], g_ref[...], sh_ref[0], sc_ref[0]).astype(BF16)
    u = _gelu(_dot(xm, w_ref[:, 0:aw]))
    va = _gelu(_dot(xm, w_ref[:, aw:2 * aw]))
    mu = jnp.mean(va, axis=-1, keepdims=True)
    d = va - mu
    var = jnp.mean(d * d, axis=-1, keepdims=True)
    vn = (d * lax.rsqrt(var + EPS) * gv_ref[...]).astype(BF16)
    gd = aw // A_GROUPS
    for n in range(tm // A_CHUNK):
        r0, r1 = n * A_CHUNK, (n + 1) * A_CHUNK
        for g in range(A_GROUPS):
            c0, c1 = g * gd, (g + 1) * gd
            s = _dot(ws_ref[g], vn[r0:r1, c0:c1]) + bs_ref[g]
            ya_ref[0, r0:r1, c0:c1] = (u[r0:r1, c0:c1] * s).astype(BF16)
    bw = q_ref.shape[-1]
    o = 2 * aw
    q_ref[0] = (_dot(xm, w_ref[:, o:o + bw]) * (HDIM ** -0.5)).astype(BF16)
    k_ref[0] = _dot(xm, w_ref[:, o + bw:o + 2 * bw]).astype(BF16)
    v_ref[0] = _dot(xm, w_ref[:, o + 2 * bw:o + 3 * bw]).astype(BF16)


def _inproj_ab(h, sh, sc, g_pre, w_in, w_s, b_s_b, g_v, aw, bw):
    b, l, d = h.shape
    tm = _tile(l, 256)
    n_in = w_in.shape[1]
    row = lambda bi, i: (bi, i, 0)
    vec = lambda bi, i: (bi, 0, 0)
    c2 = lambda bi, i: (0, 0)
    c3 = lambda bi, i: (0, 0, 0)
    body = functools.partial(_inproj_ab_body, tm=tm, aw=aw)
    return pl.pallas_call(
        body,
        grid=(b, l // tm),
        in_specs=[pl.BlockSpec((1, tm, d), row),
                  pl.BlockSpec((1, 1, d), vec), pl.BlockSpec((1, 1, d), vec),
                  pl.BlockSpec((1, d), c2),
                  pl.BlockSpec((d, n_in), c2),
                  pl.BlockSpec(w_s.shape, c3), pl.BlockSpec(b_s_b.shape, c3),
                  pl.BlockSpec((1, aw), c2)],
        out_specs=[pl.BlockSpec((1, tm, aw), row)] + [pl.BlockSpec((1, tm, bw), row)] * 3,
        out_shape=[jax.ShapeDtypeStruct((b, l, aw), BF16)] + [jax.ShapeDtypeStruct((b, l, bw), BF16)] * 3,
        compiler_params=_cparams(("parallel", "parallel"), 48),
        name="inproj_ab",
    )(h, sh, sc, g_pre, w_in, w_s, b_s_b, g_v)


def _softmax_parts(parts):
    m = parts[0].max(axis=-1, keepdims=True)
    for p in parts[1:]:
        m = jnp.maximum(m, p.max(axis=-1, keepdims=True))
    es = [jnp.exp(p - m) for p in parts]
    tot = es[0].sum(axis=-1, keepdims=True)
    for e in es[1:]:
        tot = tot + e.sum(axis=-1, keepdims=True)
    return es, tot


def _na_body(q_ref, k_ref, v_ref, kc_ref, vc_ref, t_ref, o_ref, *, rows):
    rb = pl.program_id(2)
    tq = NA_QROWS * GRID_W
    nk = NA_KROWS * GRID_W
    start = jnp.clip(NA_QROWS * rb - NA_ROWS // 2, 0, rows - NA_KROWS)
    koff = pl.multiple_of(start * GRID_W, NA_QROWS * GRID_W)
    kw = k_ref[0, pl.ds(koff, nk), :]
    vw = v_ref[0, pl.ds(koff, nk), :]
    kc = kc_ref[0]
    vc = vc_ref[0]
    q = q_ref[0]
    lo_q = lax.broadcasted_iota(jnp.int32, (tq, LANES), 1) < HDIM
    lo_b = lax.broadcasted_iota(jnp.int32, (GRID_W, LANES), 1) < GRID_W
    idx = []
    for ri in range(NA_QROWS):
        r = NA_QROWS * rb + ri
        r0 = jnp.clip(r - NA_ROWS // 2, 0, rows - NA_ROWS)
        row = []
        for j in range(NA_KROWS):
            kr = start + j
            valid = jnp.logical_and(kr >= r0, kr < r0 + NA_ROWS)
            row.append(jnp.where(valid, kr - r + NA_ROWS - 1, 2 * NA_ROWS - 1))
        idx.append(row)
    outs = []
    for a in range(2):
        qa = jnp.where(lo_q if a == 0 else jnp.logical_not(lo_q), q, jnp.zeros_like(q))
        s_loc = _dot_t(qa, kw)
        s_ctx = _dot_t(qa, kc)
        bias = jnp.concatenate([
            jnp.concatenate([
                jnp.where(lo_b, t_ref[a, idx[ri][2 * m]], t_ref[a, idx[ri][2 * m + 1]])
                for m in range(NA_KROWS // 2)], axis=1)
            for ri in range(NA_QROWS)], axis=0)
        (e_loc, e_ctx), tot = _softmax_parts([s_loc + bias, s_ctx])
        o = _dot(e_loc.astype(BF16), vw) + _dot(e_ctx.astype(BF16), vc)
        outs.append(o / tot)
    o_ref[0] = jnp.where(lo_q, outs[0], outs[1]).astype(BF16)


def _na_attention(q, k, v, kc, vc, tdup):
    b, s, w = q.shape
    lc = kc.shape[1]
    rows = s // GRID_W
    tq = NA_QROWS * GRID_W
    pairs = w // LANES
    body = functools.partial(_na_body, rows=rows)
    return pl.pallas_call(
        body,
        grid=(b, pairs, rows // NA_QROWS),
        in_specs=[pl.BlockSpec((1, tq, LANES), lambda bi, p, r: (bi, r, p)),
                  pl.BlockSpec((1, s, LANES), lambda bi, p, r: (bi, 0, p)),
                  pl.BlockSpec((1, s, LANES), lambda bi, p, r: (bi, 0, p)),
                  pl.BlockSpec((1, lc, LANES), lambda bi, p, r: (bi, 0, p)),
                  pl.BlockSpec((1, lc, LANES), lambda bi, p, r: (bi, 0, p)),
                  pl.BlockSpec((2,) + tdup.shape[1:], lambda bi, p, r: (p, 0, 0, 0))],
        out_specs=pl.BlockSpec((1, tq, LANES), lambda bi, p, r: (bi, r, p)),
        out_shape=jax.ShapeDtypeStruct((b, s, w), BF16),
        compiler_params=_cparams(("parallel", "parallel", "arbitrary"), 40),
        name="na_attention",
    )(q, k, v, kc, vc, tdup)


def _ctx_attn_body(q_ref, k_ref, v_ref, o_ref):
    q = q_ref[0]
    k = k_ref[0]
    v = v_ref[0]
    lo = lax.broadcasted_iota(jnp.int32, q.shape, 1) < HDIM
    outs = []
    for a in range(2):
        qa = jnp.where(lo if a == 0 else jnp.logical_not(lo), q, jnp.zeros_like(q))
        (e,), tot = _softmax_parts([_dot_t(qa, k)])
        outs.append(_dot(e.astype(BF16), v) / tot)
    o_ref[0] = jnp.where(lo, outs[0], outs[1]).astype(BF16)


def _ctx_attention(q, k, v):
    b, l, w = q.shape
    spec = pl.BlockSpec((1, l, LANES), lambda bi, p: (bi, 0, p))
    return pl.pallas_call(
        _ctx_attn_body,
        grid=(b, w // LANES),
        in_specs=[spec, spec, spec],
        out_specs=spec,
        out_shape=jax.ShapeDtypeStruct((b, l, w), BF16),
        compiler_params=_cparams(("parallel", "parallel"), 32),
        name="ctx_attention",
    )(q, k, v)


def _outproj_body(*refs, n_parts):
    y_refs = refs[:n_parts]
    w_refs = refs[n_parts:2 * n_parts]
    h_ref, gt_ref, g_ref, o_ref = refs[2 * n_parts:]
    y = _dot(y_refs[0][0], w_refs[0][...])
    for yr, wr in zip(y_refs[1:], w_refs[1:]):
        y = y + _dot(yr[0], wr[...])
    o_ref[0] = h_ref[0] + gt_ref[0] * _rms(y, g_ref[...])


def _outproj(ys, ws, h, gt, g_post):
    b, l, d = h.shape
    tm = _tile(l, 512)
    row = lambda bi, i: (bi, i, 0)
    vec = lambda bi, i: (bi, 0, 0)
    c2 = lambda bi, i: (0, 0)
    body = functools.partial(_outproj_body, n_parts=len(ys))
    return pl.pallas_call(
        body,
        grid=(b, l // tm),
        in_specs=([pl.BlockSpec((1, tm, y.shape[-1]), row) for y in ys]
                  + [pl.BlockSpec(w.shape, c2) for w in ws]
                  + [pl.BlockSpec((1, tm, d), row), pl.BlockSpec((1, 1, d), vec), pl.BlockSpec((1, d), c2)]),
        out_specs=pl.BlockSpec((1, tm, d), row),
        out_shape=jax.ShapeDtypeStruct((b, l, d), F32),
        compiler_params=_cparams(("parallel", "parallel"), 40),
        name="outproj",
    )(*ys, *ws, h, gt, g_post)


def _ffn_body(xp_ref, x_ref, xn_ref, sh_ref, sc_ref, gt_ref, gpre_ref, gpost_ref,
              wa_ref, wg_ref, cwa_ref, cwg_ref, cba_ref, cbg_ref, wd_ref,
              o_ref, xs_ref, p3_ref, act_ref, acc_ref, *, tm, n_tiles, n_chunks, groups):
    i = pl.program_id(1)
    g = gpre_ref[...]
    sh = sh_ref[0]
    sc = sc_ref[0]
    nv = tm // SUBLANES
    nl = x_ref.shape[-1] // LANES
    xm = _modulate(x_ref[0], g, sh, sc)
    for j in range(nl):
        for s in range(SUBLANES):
            p3_ref[j, pl.ds(s, nv, stride=SUBLANES), :] = xm[s * nv:(s + 1) * nv, j * LANES:(j + 1) * LANES]
    xs_ref[0:tm] = jnp.concatenate([p3_ref[j] for j in range(nl)], axis=1).astype(BF16)
    rid = lax.broadcasted_iota(jnp.int32, (2 * SUBLANES, 1), 0)
    keep = jnp.where(rid == 0, (i > 0).astype(F32), jnp.where(rid == 1, (i < n_tiles - 1).astype(F32), 0.0))
    halo = jnp.concatenate([xp_ref[0, SUBLANES - 1:SUBLANES], xn_ref[0, 0:1],
                            jnp.zeros((2 * SUBLANES - 2, x_ref.shape[-1]), F32)], axis=0)
    xs_ref[tm:tm + 2 * SUBLANES] = (_modulate(halo, g, sh, sc) * keep).astype(BF16)

    srow = lax.broadcasted_iota(jnp.int32, (SUBLANES, FFN_CW), 0)

    def conv(hh, cw, cb):
        hm = hh[0:tm]
        first = jnp.where(srow == 0, hh[tm:tm + 1], pltpu.roll(hm[tm - SUBLANES:tm], 1, axis=0))
        last = jnp.where(srow == SUBLANES - 1, hh[tm + 1:tm + 2], pltpu.roll(hm[0:SUBLANES], SUBLANES - 1, axis=0))
        prev = jnp.concatenate([first, hm[0:tm - SUBLANES]], axis=0)
        nxt = jnp.concatenate([hm[SUBLANES:tm], last], axis=0)
        return prev * cw[0:1] + hm * cw[1:2] + nxt * cw[2:3] + cb

    def up(c):
        xs = xs_ref[...]
        return _dot(xs, wa_ref[c]), _dot(xs, wg_ref[c])

    pending = up(0)
    k0 = 0
    for c in range(n_chunks):
        ha, hg = pending
        if c + 1 < n_chunks:
            pending = up(c + 1)
        a = conv(ha, cwa_ref[c], cba_ref[c])
        gg = conv(hg, cwg_ref[c], cbg_ref[c])
        act_ref[:, c * FFN_CW:(c + 1) * FFN_CW] = (_silu(gg) * a).astype(BF16)
        if c + 1 in groups:
            k1 = (c + 1) * FFN_CW
            part = _dot(act_ref[:, k0:k1], wd_ref[k0:k1, :])
            if k0 == 0:
                acc_ref[...] = part
            else:
                acc_ref[...] += part
            k0 = k1
    f = gt_ref[0] * _rms(acc_ref[...], gpost_ref[...])
    for j in range(nl):
        p3_ref[j] = f[:, j * LANES:(j + 1) * LANES]
    for s in range(SUBLANES):
        fs = jnp.concatenate([p3_ref[j, pl.ds(s, nv, stride=SUBLANES), :] for j in range(nl)], axis=1)
        o_ref[0, s * nv:(s + 1) * nv, :] = x_ref[0, s * nv:(s + 1) * nv, :] + fs


def _ffn(h, sh, sc, gt, g_pre, g_post, wts):
    wa, wg, cwa, cwg, cba, cbg, wd = wts
    b, l, d = h.shape
    tm = _tile(l, 512)
    n_tiles = l // tm
    n_chunks = wa.shape[0]
    ff = wd.shape[0]
    groups = (4, 8, n_chunks)
    hb = tm // SUBLANES
    last_hb = l // SUBLANES - 1
    row = lambda bi, i: (bi, i, 0)
    vec = lambda bi, i: (bi, 0, 0)
    c2 = lambda bi, i: (0, 0)
    whole = lambda a: pl.BlockSpec(a.shape, lambda bi, i: (0,) * a.ndim, pipeline_mode=pl.Buffered(1))
    body = functools.partial(_ffn_body, tm=tm, n_tiles=n_tiles, n_chunks=n_chunks, groups=groups)
    return pl.pallas_call(
        body,
        grid=(b, n_tiles),
        in_specs=[pl.BlockSpec((1, SUBLANES, d), lambda bi, i: (bi, jnp.maximum(i * hb - 1, 0), 0)),
                  pl.BlockSpec((1, tm, d), row),
                  pl.BlockSpec((1, SUBLANES, d), lambda bi, i: (bi, jnp.minimum((i + 1) * hb, last_hb), 0)),
                  pl.BlockSpec((1, 1, d), vec), pl.BlockSpec((1, 1, d), vec), pl.BlockSpec((1, 1, d), vec),
                  pl.BlockSpec((1, d), c2), pl.BlockSpec((1, d), c2),
                  whole(wa), whole(wg), whole(cwa), whole(cwg), whole(cba), whole(cbg), whole(wd)],
        out_specs=pl.BlockSpec((1, tm, d), row),
        out_shape=jax.ShapeDtypeStruct((b, l, d), F32),
        scratch_shapes=[pltpu.VMEM((tm + 2 * SUBLANES, d), BF16), pltpu.VMEM((d // LANES, tm, LANES), F32),
                        pltpu.VMEM((tm, ff), BF16), pltpu.VMEM((tm, d), F32)],
        compiler_params=_cparams(("parallel", "parallel"), 56),
        name="conv_ffn",
    )(h, h, h, sh, sc, gt, g_pre, g_post, wa, wg, cwa, cwg, cba, cbg, wd)


def _ffn_weights(w_up, conv_w, conv_b, w_down):
    d, two_ff = w_up.shape
    ff = two_ff // 2
    nch = ff // FFN_CW
    cols = lambda w: w.reshape(w.shape[0], nch, FFN_CW).transpose(1, 0, 2)
    return (cols(w_up[:, :ff].astype(BF16)), cols(w_up[:, ff:].astype(BF16)),
            cols(conv_w[:, :ff]), cols(conv_w[:, ff:]),
            cols(conv_b[None, :ff]), cols(conv_b[None, ff:]),
            w_down.astype(BF16))


def _inproj_c_body(h_ref, sh_ref, sc_ref, g_ref, wq_ref, wk_ref, wv_ref, gq_ref, gk_ref, bd_ref,
                   cos_ref, sin_ref, *out_refs, n_qtiles):
    xm = _modulate(h_ref[0], g_ref[...], sh_ref[0], sc_ref[0]).astype(BF16)
    cs = cos_ref[...]
    sn = sin_ref[...]
    bd = bd_ref[...]

    def norm_rope(w_ref, o_ref, c0, gains):
        t1 = _dot(xm, w_ref[:, c0:c0 + LANES])
        t2 = _dot(xm, w_ref[:, c0 + LANES:c0 + 2 * LANES])
        ss = _dot((t1 * t1 + t2 * t2).astype(BF16), bd)
        r = lax.rsqrt(ss * (1.0 / HDIM) + EPS)
        a1 = t1 * r * gains[0:1]
        a2 = t2 * r * gains[1:2]
        o_ref[0, :, c0:c0 + LANES] = (a1 * cs - a2 * sn).astype(BF16)
        o_ref[0, :, c0 + LANES:c0 + 2 * LANES] = (a1 * sn + a2 * cs).astype(BF16)

    if n_qtiles:
        q_ref, k_ref, v_ref = out_refs
        for jj in range(n_qtiles):
            norm_rope(wq_ref, q_ref, 2 * LANES * jj, gq_ref[...])
    else:
        k_ref, v_ref = out_refs
    norm_rope(wk_ref, k_ref, 0, gk_ref[...])
    lo = lax.broadcasted_iota(jnp.int32, (xm.shape[0], LANES), 1) < HDIM
    for t in range(wv_ref.shape[1] // LANES):
        vt = _dot(xm, wv_ref[:, t * LANES:(t + 1) * LANES])
        v_ref[0, :, (2 * t) * LANES:(2 * t + 1) * LANES] = jnp.where(lo, vt, 1.0).astype(BF16)
        v_ref[0, :, (2 * t + 1) * LANES:(2 * t + 2) * LANES] = jnp.where(lo, 1.0, vt).astype(BF16)


def _inproj_c(h, sh, sc, g_pre, wq, wk, wv, gq, gk, bd, cos, sin, with_q):
    b, l, d = h.shape
    tm = _tile(l, 256)
    row = lambda bi, i: (bi, i, 0)
    vec = lambda bi, i: (bi, 0, 0)
    c2 = lambda bi, i: (0, 0)
    nq, nkv = wq.shape[1], wk.shape[1]
    n_qtiles = nq // (2 * LANES) if with_q else 0
    body = functools.partial(_inproj_c_body, n_qtiles=n_qtiles)
    out_specs = [pl.BlockSpec((1, tm, nkv), row), pl.BlockSpec((1, tm, 2 * nkv), row)]
    out_shape = [jax.ShapeDtypeStruct((b, l, nkv), BF16), jax.ShapeDtypeStruct((b, l, 2 * nkv), BF16)]
    if with_q:
        out_specs = [pl.BlockSpec((1, tm, nq), row)] + out_specs
        out_shape = [jax.ShapeDtypeStruct((b, l, nq), BF16)] + out_shape
    return pl.pallas_call(
        body,
        grid=(b, l // tm),
        in_specs=[pl.BlockSpec((1, tm, d), row),
                  pl.BlockSpec((1, 1, d), vec), pl.BlockSpec((1, 1, d), vec),
                  pl.BlockSpec((1, d), c2),
                  pl.BlockSpec(wq.shape, c2), pl.BlockSpec(wk.shape, c2), pl.BlockSpec(wv.shape, c2),
                  pl.BlockSpec(gq.shape, c2), pl.BlockSpec(gk.shape, c2), pl.BlockSpec(bd.shape, c2),
                  pl.BlockSpec((tm, LANES), lambda bi, i: (i, 0)),
                  pl.BlockSpec((tm, LANES), lambda bi, i: (i, 0))],
        out_specs=out_specs,
        out_shape=out_shape,
        compiler_params=_cparams(("parallel", "parallel"), 40),
        name="inproj_c",
    )(h, sh, sc, g_pre, wq, wk, wv, gq, gk, bd, cos, sin)


def _gqa_body(q_ref, k_ref, v_ref, kc_ref, vc_ref, o_ref, *, tq, n_sub):
    k = k_ref[0]
    kc = kc_ref[0]
    seg = (lax.broadcasted_iota(jnp.int32, (tq, 2 * LANES), 1) % LANES) // (HDIM // 2)
    lo = lax.broadcasted_iota(jnp.int32, (tq, LANES), 1) < HDIM
    for t in range(n_sub):
        q = q_ref[0, t * tq:(t + 1) * tq]
        zero = jnp.zeros_like(q)
        lhs = jnp.concatenate([jnp.where(seg == s, q, zero) for s in range(4)], axis=0)
        s_lat = _dot_t(lhs, k)
        s_ctx = _dot_t(lhs, kc)
        m = jnp.maximum(s_lat.max(axis=-1, keepdims=True), s_ctx.max(axis=-1, keepdims=True))
        p_lat = jnp.exp2(s_lat - m).astype(BF16)
        p_ctx = jnp.exp2(s_ctx - m).astype(BF16)
        outs = []
        for s in range(4):
            rs = slice(s * tq, (s + 1) * tq)
            cs = slice(s * LANES, (s + 1) * LANES)
            o = _dot(p_lat[rs], v_ref[0, :, cs]) + _dot(p_ctx[rs], vc_ref[0, :, cs])
            outs.append(o / pltpu.roll(o, HDIM, axis=1))
        o_ref[0, t * tq:(t + 1) * tq] = jnp.concatenate(
            [jnp.where(lo, outs[0], outs[1]), jnp.where(lo, outs[2], outs[3])], axis=1).astype(BF16)


def _gqa_attention(q, k, v, kc, vc):
    b, s, nq = q.shape
    lc = kc.shape[1]
    nkv = k.shape[-1]
    n_sub = 4
    tq = 128 * n_sub
    body = functools.partial(_gqa_body, tq=tq // n_sub, n_sub=n_sub)
    kv_spec = lambda n, w: pl.BlockSpec((1, n, w), lambda bi, i, j: (bi, 0, 0))
    return pl.pallas_call(
        body,
        grid=(b, s // tq, nq // (2 * LANES)),
        in_specs=[pl.BlockSpec((1, tq, 2 * LANES), lambda bi, i, j: (bi, i, j)),
                  kv_spec(s, nkv), kv_spec(s, 2 * nkv), kv_spec(lc, nkv), kv_spec(lc, 2 * nkv)],
        out_specs=pl.BlockSpec((1, tq, 2 * LANES), lambda bi, i, j: (bi, i, j)),
        out_shape=jax.ShapeDtypeStruct((b, s, nq), BF16),
        compiler_params=_cparams(("parallel", "arbitrary", "arbitrary"), 48),
        name="gqa_attention",
    )(q, k, v, kc, vc)


def _na_bias_table(rpb):
    qc = np.arange(GRID_W)[:, None]
    kc = np.arange(GRID_W)[None, :]
    c0 = np.clip(qc - NA_COLS // 2, 0, GRID_W - NA_COLS)
    in_win = (kc >= c0) & (kc < c0 + NA_COLS)
    dc = np.clip(kc - qc, -(NA_COLS - 1), NA_COLS - 1) + NA_COLS - 1
    t = jnp.where(in_win, rpb[:, :, dc], NEG_INF)
    t = jnp.concatenate([t, jnp.full_like(t[:, :1], NEG_INF)], axis=1)
    return jnp.concatenate([t, t], axis=-1)


def _gqa_layout(n_heads, n_kv):
    half = HDIM // 2
    q_cols = [(n_kv * s + jj) * HDIM + p * half + i
              for jj in range(n_heads // n_kv) for p in range(2) for s in range(n_kv) for i in range(half)]
    k_cols = [s * HDIM + p * half + i for p in range(2) for s in range(n_kv) for i in range(half)]
    y_rows = [(n_kv * s + jj) * HDIM + c
              for jj in range(n_heads // n_kv) for s in range(n_kv) for c in range(HDIM)]
    return np.array(q_cols), np.array(k_cols), np.array(y_rows)


def _rope_tables(n_tokens):
    t = jnp.arange(n_tokens)
    quarter = HDIM // 4
    inv = ROPE_THETA ** (-jnp.arange(quarter, dtype=F32) / quarter)
    rows = (t // GRID_W).astype(F32)[:, None] * inv
    cols = (t % GRID_W).astype(F32)[:, None] * inv
    ang = jnp.tile(jnp.concatenate([rows, cols], axis=-1), (1, LANES // (HDIM // 2)))
    return jnp.cos(ang), jnp.sin(ang)


def kernel(x, c, ctx, c_ctx, w_ada, b_ada, norm_g, w_in_ab, a_w_s, a_b_s, a_v_g, b_rpb, w_out_ab,
           w_qkv_c, c_q_g, c_k_g, w_out_c, w_up, conv_w, conv_b, w_down):
    bsz, seq, d = x.shape
    lc = ctx.shape[1]
    depth = w_ada.shape[0]
    aw = a_v_g.shape[-1]
    n_heads_b = b_rpb.shape[1]
    bw = n_heads_b * HDIM
    n_kv = 4
    n_heads_c = w_out_c.shape[1] // HDIM
    assert seq % (NA_QROWS * GRID_W) == 0 and seq // GRID_W >= NA_KROWS
    assert aw // A_GROUPS == A_CHUNK and w_up.shape[-1] // 2 % FFN_CW == 0

    n_rows = 8 * ((bsz + 1 + 7) // 8)
    cv = jnp.zeros((n_rows, d), F32).at[:bsz].set(c).at[bsz].set(c_ctx)
    mods = _ada(cv, w_ada, b_ada)

    def mod_vectors(layer):
        m = mods[layer].reshape(n_rows, 6, d)
        lat = [m[:bsz, j][:, None, :] for j in range(6)]
        cx = [jnp.broadcast_to(m[bsz, j][None, None, :], (bsz, 1, d)) for j in range(6)]
        return lat, cx

    cos_l, sin_l = _rope_tables(seq)
    cos_c, sin_c = jnp.ones((lc, LANES), F32), jnp.zeros((lc, LANES), F32)

    h, hc = x, ctx
    for layer in range(depth):
        with_ctx = layer < depth - 1
        lat, cx = mod_vectors(layer)
        g_pre_m, g_post_m, g_pre_f, g_post_f = [norm_g[layer, j][None, :] for j in range(4)]
        if layer % 2 == 0:
            e = layer // 2
            w_in = w_in_ab[e].astype(BF16)
            w_s = a_w_s[e].astype(BF16)
            b_s_b = jnp.broadcast_to(a_b_s[e][:, :, None], a_w_s[e].shape[:2] + (aw // A_GROUPS,))
            g_v = a_v_g[e][None, :]
            w_out = w_out_ab[e].astype(BF16)
            ya, q, k, v = _inproj_ab(h, lat[0], lat[1], g_pre_m, w_in, w_s, b_s_b, g_v, aw, bw)
            yca, qc, kc, vc = _inproj_ab(hc, cx[0], cx[1], g_pre_m, w_in, w_s, b_s_b, g_v, aw, bw)
            yb = _na_attention(q, k, v, kc, vc, _na_bias_table(b_rpb[e]))
            h = _outproj([ya, yb], [w_out[:aw], w_out[aw:]], h, lat[2], g_post_m)
            if with_ctx:
                ycb = _ctx_attention(qc, kc, vc)
                hc = _outproj([yca, ycb], [w_out[:aw], w_out[aw:]], hc, cx[2], g_post_m)
        else:
            o = layer // 2
            q_cols, k_cols, y_rows = _gqa_layout(n_heads_c, n_kv)
            nq = n_heads_c * HDIM
            nkv = n_kv * HDIM
            wqkv = w_qkv_c[o]
            wq = wqkv[:, :nq][:, q_cols].astype(BF16)
            wk = wqkv[:, nq:nq + nkv][:, k_cols].astype(BF16)
            wv = wqkv[:, nq + nkv:].astype(BF16)
            half = HDIM // 2
            reps = LANES // half
            gq = (jnp.stack([jnp.tile(c_q_g[o][:half], reps), jnp.tile(c_q_g[o][half:], reps)])
                  * (HDIM ** -0.5 * LOG2E))
            gk = jnp.stack([jnp.tile(c_k_g[o][:half], reps), jnp.tile(c_k_g[o][half:], reps)])
            seg = np.arange(LANES) // half
            bd = jnp.asarray(seg[:, None] == seg[None, :], BF16)
            w_out = w_out_c[o][y_rows].astype(BF16)
            q, k, v = _inproj_c(h, lat[0], lat[1], g_pre_m, wq, wk, wv, gq, gk, bd, cos_l, sin_l, True)
            kc, vc = _inproj_c(hc, cx[0], cx[1], g_pre_m, wq, wk, wv, gq, gk, bd, cos_c, sin_c, False)
            y = _gqa_attention(q, k, v, kc, vc)
            h = _outproj([y], [w_out], h, lat[2], g_post_m)
            if with_ctx:
                raise NotImplementedError("context update after a grouped-query layer")
        wts = _ffn_weights(w_up[layer], conv_w[layer], conv_b[layer], w_down[layer])
        h = _ffn(h, lat[3], lat[4], lat[5], g_pre_f, g_post_f, wts)
        if with_ctx:
            hc = _ffn(hc, cx[3], cx[4], cx[5], g_pre_f, g_post_f, wts)
    return h
```

```python
import functools

import numpy as np
import jax
import jax.numpy as jnp
from jax import lax
from jax.experimental import pallas as pl
from jax.experimental.pallas import tpu as pltpu

F32 = jnp.float32
BF16 = jnp.bfloat16

EPS = 1e-6
NEG_INF = -1e30
GRID_W = 64
ROPE_THETA = 10000.0
LOG2E = 1.4426950408889634
A_CHUNK = 128
A_GROUPS = 4
NA_ROWS = 8
NA_COLS = 16
HDIM = 64
LANES = 128
NA_QROWS = 4
NA_KROWS = 12
SUBLANES = 8
FFN_CW = 256
V7X_VMEM_BYTES = 64 * 1024 * 1024


def _cparams(sem, vmem_mb):
    assert vmem_mb * 1024 * 1024 < V7X_VMEM_BYTES
    return pltpu.CompilerParams(dimension_semantics=sem, vmem_limit_bytes=vmem_mb * 1024 * 1024)


def _tile(n, pref):
    return pref if n % pref == 0 else n


def _rms(x, g):
    return x * lax.rsqrt(jnp.mean(x * x, axis=-1, keepdims=True) + EPS) * g


def _modulate(x, g, shift, scale):
    return _rms(x, g) * (1.0 + scale) + shift


def _gelu(x):
    return 0.5 * x * (1.0 + jnp.tanh(0.7978845608028654 * (x + 0.044715 * (x * x * x))))


def _silu(x):
    return x * jax.nn.sigmoid(x)


def _dot(a, b):
    return jnp.dot(a, b, preferred_element_type=F32)


def _dot_t(a, b):
    return lax.dot_general(a, b, (((1,), (1,)), ((), ())), preferred_element_type=F32)


def _ada_body(c_ref, w_ref, b_ref, o_ref):
    s = _silu(c_ref[...])
    o_ref[0] = jnp.dot(s, w_ref[0], preferred_element_type=F32,
                       precision=lax.Precision.HIGHEST) + b_ref[0]


def _ada(cv, w_ada, b_ada):
    depth, d, n = w_ada.shape
    rows = cv.shape[0]
    tn = _tile(n, 1536)
    return pl.pallas_call(
        _ada_body,
        grid=(depth, n // tn),
        in_specs=[pl.BlockSpec((rows, d), lambda l, j: (0, 0)),
                  pl.BlockSpec((1, d, tn), lambda l, j: (l, 0, j)),
                  pl.BlockSpec((1, 1, tn), lambda l, j: (l, 0, j))],
        out_specs=pl.BlockSpec((1, rows, tn), lambda l, j: (l, 0, j)),
        out_shape=jax.ShapeDtypeStruct((depth, rows, n), F32),
        compiler_params=_cparams(("parallel", "parallel"), 40),
        name="ada",
    )(cv, w_ada, b_ada.reshape(depth, 1, n))


def _store_ones_padded(v_ref, rows, vals):
    lo = lax.broadcasted_iota(jnp.int32, (vals.shape[0], LANES), 1) < HDIM
    for t in range(vals.shape[1] // LANES):
        vt = vals[:, t * LANES:(t + 1) * LANES]
        v_ref[0, rows, (2 * t) * LANES:(2 * t + 1) * LANES] = jnp.where(lo, vt, 1.0).astype(BF16)
        v_ref[0, rows, (2 * t + 1) * LANES:(2 * t + 2) * LANES] = jnp.where(lo, 1.0, vt).astype(BF16)


def _inproj_ab_body(h_ref, sh_ref, sc_ref, g_ref, w_ref, ws_ref, bs_ref, gv_ref,
                    ya_ref, q_ref, k_ref, v_ref, *, sub, n_sub, aw):
    bw = q_ref.shape[-1]
    gd = aw // A_GROUPS
    for t in range(n_sub):
        rows = slice(t * sub, (t + 1) * sub)
        xm = _modulate(h_ref[0, rows], g_ref[...], sh_ref[0], sc_ref[0]).astype(BF16)
        u = _gelu(_dot(xm, w_ref[:, 0:aw]))
        va = _gelu(_dot(xm, w_ref[:, aw:2 * aw]))
        mu = jnp.mean(va, axis=-1, keepdims=True)
        d = va - mu
        var = jnp.mean(d * d, axis=-1, keepdims=True)
        vn = (d * lax.rsqrt(var + EPS) * gv_ref[...]).astype(BF16)
        for n in range(sub // A_CHUNK):
            r0, r1 = n * A_CHUNK, (n + 1) * A_CHUNK
            for g in range(A_GROUPS):
                c0, c1 = g * gd, (g + 1) * gd
                s = _dot(ws_ref[g], vn[r0:r1, c0:c1]) + bs_ref[g]
                ya_ref[0, t * sub + r0:t * sub + r1, c0:c1] = (u[r0:r1, c0:c1] * s).astype(BF16)
        o = 2 * aw
        q_ref[0, rows] = (_dot(xm, w_ref[:, o:o + bw]) * (HDIM ** -0.5 * LOG2E)).astype(BF16)
        k_ref[0, rows] = _dot(xm, w_ref[:, o + bw:o + 2 * bw]).astype(BF16)
        _store_ones_padded(v_ref, rows, _dot(xm, w_ref[:, o + 2 * bw:o + 3 * bw]))


def _inproj_ab(h, sh, sc, g_pre, w_in, w_s, b_s_b, g_v, aw, bw):
    b, l, d = h.shape
    sub = _tile(l, 512)
    n_sub = 2 if l % (2 * sub) == 0 else 1
    tm = sub * n_sub
    n_in = w_in.shape[1]
    row = lambda bi, i: (bi, i, 0)
    vec = lambda bi, i: (bi, 0, 0)
    c2 = lambda bi, i: (0, 0)
    c3 = lambda bi, i: (0, 0, 0)
    body = functools.partial(_inproj_ab_body, sub=sub, n_sub=n_sub, aw=aw)
    widths = (aw, bw, bw, 2 * bw)
    return pl.pallas_call(
        body,
        grid=(b, l // tm),
        in_specs=[pl.BlockSpec((1, tm, d), row),
                  pl.BlockSpec((1, 1, d), vec), pl.BlockSpec((1, 1, d), vec),
                  pl.BlockSpec((1, d), c2),
                  pl.BlockSpec((d, n_in), c2),
                  pl.BlockSpec(w_s.shape, c3), pl.BlockSpec(b_s_b.shape, c3),
                  pl.BlockSpec((1, aw), c2)],
        out_specs=[pl.BlockSpec((1, tm, w), row) for w in widths],
        out_shape=[jax.ShapeDtypeStruct((b, l, w), BF16) for w in widths],
        compiler_params=_cparams(("parallel", "parallel"), 48),
        name="inproj_ab",
    )(h, sh, sc, g_pre, w_in, w_s, b_s_b, g_v)


def _softmax_pv(scores, values):
    m = scores[0].max(axis=-1, keepdims=True)
    for s in scores[1:]:
        m = jnp.maximum(m, s.max(axis=-1, keepdims=True))
    o = _dot(jnp.exp2(scores[0] - m).astype(BF16), values[0])
    for s, v in zip(scores[1:], values[1:]):
        o = o + _dot(jnp.exp2(s - m).astype(BF16), v)
    return o


def _normalised_pair(o_even, o_odd):
    lo = lax.broadcasted_iota(jnp.int32, o_even.shape, 1) < HDIM
    return jnp.where(lo, o_even / pltpu.roll(o_even, HDIM, axis=1), o_odd / pltpu.roll(o_odd, HDIM, axis=1))


def _na_body(q_ref, k_ref, v_ref, kc_ref, vc_ref, t_ref, o_ref, *, rows):
    rb = pl.program_id(1)
    tq = NA_QROWS * GRID_W
    nk = NA_KROWS * GRID_W
    start = jnp.clip(NA_QROWS * rb - NA_ROWS // 2, 0, rows - NA_KROWS)
    koff = pl.multiple_of(start * GRID_W, NA_QROWS * GRID_W)
    lo_q = lax.broadcasted_iota(jnp.int32, (tq, LANES), 1) < HDIM
    lo_b = lax.broadcasted_iota(jnp.int32, (GRID_W, LANES), 1) < GRID_W
    idx = []
    for ri in range(NA_QROWS):
        r = NA_QROWS * rb + ri
        r0 = jnp.clip(r - NA_ROWS // 2, 0, rows - NA_ROWS)
        row = []
        for j in range(NA_KROWS):
            kr = start + j
            valid = jnp.logical_and(kr >= r0, kr < r0 + NA_ROWS)
            row.append(jnp.where(valid, kr - r + NA_ROWS - 1, 2 * NA_ROWS - 1))
        idx.append(row)
    for p in range(q_ref.shape[-1] // LANES):
        cp = slice(p * LANES, (p + 1) * LANES)
        q = q_ref[0, :, cp]
        kw = k_ref[0, pl.ds(koff, nk), cp]
        kc = kc_ref[0, :, cp]
        outs = []
        for a in range(2):
            ch = slice((2 * p + a) * LANES, (2 * p + a + 1) * LANES)
            qa = jnp.where(lo_q if a == 0 else jnp.logical_not(lo_q), q, jnp.zeros_like(q))
            bias = jnp.concatenate([
                jnp.concatenate([
                    jnp.where(lo_b, t_ref[2 * p + a, idx[ri][2 * m]], t_ref[2 * p + a, idx[ri][2 * m + 1]])
                    for m in range(NA_KROWS // 2)], axis=1)
                for ri in range(NA_QROWS)], axis=0)
            outs.append(_softmax_pv([_dot_t(qa, kw) + bias, _dot_t(qa, kc)],
                                    [v_ref[0, pl.ds(koff, nk), ch], vc_ref[0, :, ch]]))
        o_ref[0, :, cp] = _normalised_pair(outs[0], outs[1]).astype(BF16)


def _na_attention(q, k, v, kc, vc, tdup):
    b, s, w = q.shape
    lc = kc.shape[1]
    rows = s // GRID_W
    tq = NA_QROWS * GRID_W
    body = functools.partial(_na_body, rows=rows)
    whole = lambda n, wd: pl.BlockSpec((1, n, wd), lambda bi, r: (bi, 0, 0))
    return pl.pallas_call(
        body,
        grid=(b, rows // NA_QROWS),
        in_specs=[pl.BlockSpec((1, tq, w), lambda bi, r: (bi, r, 0)),
                  whole(s, w), whole(s, 2 * w), whole(lc, w), whole(lc, 2 * w),
                  pl.BlockSpec(tdup.shape, lambda bi, r: (0, 0, 0, 0))],
        out_specs=pl.BlockSpec((1, tq, w), lambda bi, r: (bi, r, 0)),
        out_shape=jax.ShapeDtypeStruct((b, s, w), BF16),
        compiler_params=_cparams(("parallel", "arbitrary"), 48),
        name="na_attention",
    )(q, k, v, kc, vc, tdup)


def _ctx_attn_body(q_ref, k_ref, v_ref, o_ref):
    q = q_ref[0]
    k = k_ref[0]
    lo = lax.broadcasted_iota(jnp.int32, q.shape, 1) < HDIM
    outs = []
    for a in range(2):
        qa = jnp.where(lo if a == 0 else jnp.logical_not(lo), q, jnp.zeros_like(q))
        outs.append(_softmax_pv([_dot_t(qa, k)], [v_ref[0, :, a * LANES:(a + 1) * LANES]]))
    o_ref[0] = _normalised_pair(outs[0], outs[1]).astype(BF16)


def _ctx_attention(q, k, v):
    b, l, w = q.shape
    spec = pl.BlockSpec((1, l, LANES), lambda bi, p: (bi, 0, p))
    return pl.pallas_call(
        _ctx_attn_body,
        grid=(b, w // LANES),
        in_specs=[spec, spec, pl.BlockSpec((1, l, 2 * LANES), lambda bi, p: (bi, 0, p))],
        out_specs=spec,
        out_shape=jax.ShapeDtypeStruct((b, l, w), BF16),
        compiler_params=_cparams(("parallel", "parallel"), 32),
        name="ctx_attention",
    )(q, k, v)


def _outproj_body(*refs, n_parts):
    y_refs = refs[:n_parts]
    w_refs = refs[n_parts:2 * n_parts]
    h_ref, gt_ref, g_ref, o_ref = refs[2 * n_parts:]
    y = _dot(y_refs[0][0], w_refs[0][...])
    for yr, wr in zip(y_refs[1:], w_refs[1:]):
        y = y + _dot(yr[0], wr[...])
    o_ref[0] = h_ref[0] + gt_ref[0] * _rms(y, g_ref[...])


def _outproj(ys, ws, h, gt, g_post):
    b, l, d = h.shape
    tm = _tile(l, 512)
    row = lambda bi, i: (bi, i, 0)
    vec = lambda bi, i: (bi, 0, 0)
    c2 = lambda bi, i: (0, 0)
    body = functools.partial(_outproj_body, n_parts=len(ys))
    return pl.pallas_call(
        body,
        grid=(b, l // tm),
        in_specs=([pl.BlockSpec((1, tm, y.shape[-1]), row) for y in ys]
                  + [pl.BlockSpec(w.shape, c2) for w in ws]
                  + [pl.BlockSpec((1, tm, d), row), pl.BlockSpec((1, 1, d), vec), pl.BlockSpec((1, d), c2)]),
        out_specs=pl.BlockSpec((1, tm, d), row),
        out_shape=jax.ShapeDtypeStruct((b, l, d), F32),
        compiler_params=_cparams(("parallel", "parallel"), 40),
        name="outproj",
    )(*ys, *ws, h, gt, g_post)


def _ffn_body(xp_ref, x_ref, xn_ref, sh_ref, sc_ref, gt_ref, gpre_ref, gpost_ref,
              wa_ref, wg_ref, cwa_ref, cwg_ref, cba_ref, cbg_ref, wd_ref,
              o_ref, xs_ref, p3_ref, act_ref, acc_ref, *, tm, n_tiles, n_chunks, groups):
    i = pl.program_id(1)
    g = gpre_ref[...]
    sh = sh_ref[0]
    sc = sc_ref[0]
    nv = tm // SUBLANES
    nl = x_ref.shape[-1] // LANES
    xm = _modulate(x_ref[0], g, sh, sc)
    for j in range(nl):
        for s in range(SUBLANES):
            p3_ref[j, pl.ds(s, nv, stride=SUBLANES), :] = xm[s * nv:(s + 1) * nv, j * LANES:(j + 1) * LANES]
    xs_ref[0:tm] = jnp.concatenate([p3_ref[j] for j in range(nl)], axis=1).astype(BF16)
    rid = lax.broadcasted_iota(jnp.int32, (2 * SUBLANES, 1), 0)
    keep = jnp.where(rid == 0, (i > 0).astype(F32), jnp.where(rid == 1, (i < n_tiles - 1).astype(F32), 0.0))
    halo = jnp.concatenate([xp_ref[0, SUBLANES - 1:SUBLANES], xn_ref[0, 0:1],
                            jnp.zeros((2 * SUBLANES - 2, x_ref.shape[-1]), F32)], axis=0)
    xs_ref[tm:tm + 2 * SUBLANES] = (_modulate(halo, g, sh, sc) * keep).astype(BF16)

    srow = lax.broadcasted_iota(jnp.int32, (SUBLANES, FFN_CW), 0)

    def conv(hh, cw, cb):
        hm = hh[0:tm]
        first = jnp.where(srow == 0, hh[tm:tm + 1], pltpu.roll(hm[tm - SUBLANES:tm], 1, axis=0))
        last = jnp.where(srow == SUBLANES - 1, hh[tm + 1:tm + 2], pltpu.roll(hm[0:SUBLANES], SUBLANES - 1, axis=0))
        prev = jnp.concatenate([first, hm[0:tm - SUBLANES]], axis=0)
        nxt = jnp.concatenate([hm[SUBLANES:tm], last], axis=0)
        return prev * cw[0:1] + hm * cw[1:2] + nxt * cw[2:3] + cb

    def up(c):
        xs = xs_ref[...]
        cc = slice(c * FFN_CW, (c + 1) * FFN_CW)
        return _dot(xs, wa_ref[:, cc]), _dot(xs, wg_ref[:, cc])

    pending = up(0)
    k0 = 0
    for c in range(n_chunks):
        ha, hg = pending
        if c + 1 < n_chunks:
            pending = up(c + 1)
        cc = slice(c * FFN_CW, (c + 1) * FFN_CW)
        a = conv(ha, cwa_ref[:, cc], cba_ref[:, cc])
        gg = conv(hg, cwg_ref[:, cc], cbg_ref[:, cc])
        act_ref[:, cc] = (_silu(gg) * a).astype(BF16)
        if c + 1 in groups:
            k1 = (c + 1) * FFN_CW
            part = _dot(act_ref[:, k0:k1], wd_ref[k0:k1, :])
            if k0 == 0:
                acc_ref[...] = part
            else:
                acc_ref[...] += part
            k0 = k1
    f = gt_ref[0] * _rms(acc_ref[...], gpost_ref[...])
    for j in range(nl):
        p3_ref[j] = f[:, j * LANES:(j + 1) * LANES]
    for s in range(SUBLANES):
        fs = jnp.concatenate([p3_ref[j, pl.ds(s, nv, stride=SUBLANES), :] for j in range(nl)], axis=1)
        o_ref[0, s * nv:(s + 1) * nv, :] = x_ref[0, s * nv:(s + 1) * nv, :] + fs


def _ffn(h, sh, sc, gt, g_pre, g_post, wts):
    wa, wg, cwa, cwg, cba, cbg, wd = wts
    b, l, d = h.shape
    tm = _tile(l, 512)
    n_tiles = l // tm
    n_chunks = wa.shape[1] // FFN_CW
    ff = wd.shape[0]
    groups = (4, 8, n_chunks)
    hb = tm // SUBLANES
    last_hb = l // SUBLANES - 1
    row = lambda bi, i: (bi, i, 0)
    vec = lambda bi, i: (bi, 0, 0)
    c2 = lambda bi, i: (0, 0)
    whole = lambda a: pl.BlockSpec(a.shape, lambda bi, i: (0,) * a.ndim, pipeline_mode=pl.Buffered(1))
    body = functools.partial(_ffn_body, tm=tm, n_tiles=n_tiles, n_chunks=n_chunks, groups=groups)
    return pl.pallas_call(
        body,
        grid=(b, n_tiles),
        in_specs=[pl.BlockSpec((1, SUBLANES, d), lambda bi, i: (bi, jnp.maximum(i * hb - 1, 0), 0)),
                  pl.BlockSpec((1, tm, d), row),
                  pl.BlockSpec((1, SUBLANES, d), lambda bi, i: (bi, jnp.minimum((i + 1) * hb, last_hb), 0)),
                  pl.BlockSpec((1, 1, d), vec), pl.BlockSpec((1, 1, d), vec), pl.BlockSpec((1, 1, d), vec),
                  pl.BlockSpec((1, d), c2), pl.BlockSpec((1, d), c2),
                  whole(wa), whole(wg), whole(cwa), whole(cwg), whole(cba), whole(cbg), whole(wd)],
        out_specs=pl.BlockSpec((1, tm, d), row),
        out_shape=jax.ShapeDtypeStruct((b, l, d), F32),
        scratch_shapes=[pltpu.VMEM((tm + 2 * SUBLANES, d), BF16), pltpu.VMEM((d // LANES, tm, LANES), F32),
                        pltpu.VMEM((tm, ff), BF16), pltpu.VMEM((tm, d), F32)],
        compiler_params=_cparams(("parallel", "parallel"), 56),
        name="conv_ffn",
    )(h, h, h, sh, sc, gt, g_pre, g_post, wa, wg, cwa, cwg, cba, cbg, wd)


def _ffn_weights(w_up, conv_w, conv_b, w_down):
    ff = w_up.shape[1] // 2
    return (w_up[:, :ff].astype(BF16), w_up[:, ff:].astype(BF16), conv_w[:, :ff], conv_w[:, ff:],
            conv_b[None, :ff], conv_b[None, ff:], w_down.astype(BF16))


def _inproj_c_body(h_ref, sh_ref, sc_ref, g_ref, wq_ref, wk_ref, wv_ref, gq_ref, gk_ref, bd_ref,
                   cos_ref, sin_ref, *out_refs, n_qtiles, sub, n_sub):
    bd = bd_ref[...]
    if n_qtiles:
        q_ref, k_ref, v_ref = out_refs
    else:
        k_ref, v_ref = out_refs
    for t in range(n_sub):
        rows = slice(t * sub, (t + 1) * sub)
        xm = _modulate(h_ref[0, rows], g_ref[...], sh_ref[0], sc_ref[0]).astype(BF16)
        cs = cos_ref[rows]
        sn = sin_ref[rows]

        def norm_rope(w_ref, o_ref, c0, gains):
            tt = _dot(xm, w_ref[:, c0:c0 + 2 * LANES])
            t1 = tt[:, :LANES]
            t2 = tt[:, LANES:]
            ss = _dot((t1 * t1 + t2 * t2).astype(BF16), bd)
            r = lax.rsqrt(ss * (1.0 / HDIM) + EPS)
            a1 = t1 * r * gains[0:1]
            a2 = t2 * r * gains[1:2]
            o_ref[0, rows, c0:c0 + LANES] = (a1 * cs - a2 * sn).astype(BF16)
            o_ref[0, rows, c0 + LANES:c0 + 2 * LANES] = (a1 * sn + a2 * cs).astype(BF16)

        for jj in range(n_qtiles):
            norm_rope(wq_ref, q_ref, 2 * LANES * jj, gq_ref[...])
        norm_rope(wk_ref, k_ref, 0, gk_ref[...])
        _store_ones_padded(v_ref, rows, _dot(xm, wv_ref[...]))


def _inproj_c(h, sh, sc, g_pre, wq, wk, wv, gq, gk, bd, cos, sin, with_q):
    b, l, d = h.shape
    sub = _tile(l, 512)
    n_sub = 2 if l % (2 * sub) == 0 else 1
    tm = sub * n_sub
    row = lambda bi, i: (bi, i, 0)
    vec = lambda bi, i: (bi, 0, 0)
    c2 = lambda bi, i: (0, 0)
    nq, nkv = wq.shape[1], wk.shape[1]
    n_qtiles = nq // (2 * LANES) if with_q else 0
    body = functools.partial(_inproj_c_body, n_qtiles=n_qtiles, sub=sub, n_sub=n_sub)
    out_specs = [pl.BlockSpec((1, tm, nkv), row), pl.BlockSpec((1, tm, 2 * nkv), row)]
    out_shape = [jax.ShapeDtypeStruct((b, l, nkv), BF16), jax.ShapeDtypeStruct((b, l, 2 * nkv), BF16)]
    if with_q:
        out_specs = [pl.BlockSpec((1, tm, nq), row)] + out_specs
        out_shape = [jax.ShapeDtypeStruct((b, l, nq), BF16)] + out_shape
    return pl.pallas_call(
        body,
        grid=(b, l // tm),
        in_specs=[pl.BlockSpec((1, tm, d), row),
                  pl.BlockSpec((1, 1, d), vec), pl.BlockSpec((1, 1, d), vec),
                  pl.BlockSpec((1, d), c2),
                  pl.BlockSpec(wq.shape, c2), pl.BlockSpec(wk.shape, c2), pl.BlockSpec(wv.shape, c2),
                  pl.BlockSpec(gq.shape, c2), pl.BlockSpec(gk.shape, c2), pl.BlockSpec(bd.shape, c2),
                  pl.BlockSpec((tm, LANES), lambda bi, i: (i, 0)),
                  pl.BlockSpec((tm, LANES), lambda bi, i: (i, 0))],
        out_specs=out_specs,
        out_shape=out_shape,
        compiler_params=_cparams(("parallel", "parallel"), 40),
        name="inproj_c",
    )(h, sh, sc, g_pre, wq, wk, wv, gq, gk, bd, cos, sin)


def _gqa_body(q_ref, k_ref, v_ref, kc_ref, vc_ref, o_ref, *, tq, n_sub):
    k = k_ref[0]
    kc = kc_ref[0]
    seg = (lax.broadcasted_iota(jnp.int32, (tq, 2 * LANES), 1) % LANES) // (HDIM // 2)
    lo = lax.broadcasted_iota(jnp.int32, (tq, LANES), 1) < HDIM
    for t in range(n_sub):
        q = q_ref[0, t * tq:(t + 1) * tq]
        zero = jnp.zeros_like(q)
        lhs = jnp.concatenate([jnp.where(seg == s, q, zero) for s in range(4)], axis=0)
        s_lat = _dot_t(lhs, k)
        s_ctx = _dot_t(lhs, kc)
        m = jnp.maximum(s_lat.max(axis=-1, keepdims=True), s_ctx.max(axis=-1, keepdims=True))
        p_lat = jnp.exp2(s_lat - m).astype(BF16)
        p_ctx = jnp.exp2(s_ctx - m).astype(BF16)
        outs = []
        for s in range(4):
            rs = slice(s * tq, (s + 1) * tq)
            cs = slice(s * LANES, (s + 1) * LANES)
            outs.append(_dot(p_lat[rs], v_ref[0, :, cs]) + _dot(p_ctx[rs], vc_ref[0, :, cs]))
        o_ref[0, t * tq:(t + 1) * tq] = jnp.concatenate(
            [_normalised_pair(outs[0], outs[1]), _normalised_pair(outs[2], outs[3])], axis=1).astype(BF16)


def _gqa_attention(q, k, v, kc, vc):
    b, s, nq = q.shape
    lc = kc.shape[1]
    nkv = k.shape[-1]
    n_sub = 4
    tq = 128 * n_sub
    body = functools.partial(_gqa_body, tq=tq // n_sub, n_sub=n_sub)
    kv_spec = lambda n, w: pl.BlockSpec((1, n, w), lambda bi, i, j: (bi, 0, 0))
    return pl.pallas_call(
        body,
        grid=(b, s // tq, nq // (2 * LANES)),
        in_specs=[pl.BlockSpec((1, tq, 2 * LANES), lambda bi, i, j: (bi, i, j)),
                  kv_spec(s, nkv), kv_spec(s, 2 * nkv), kv_spec(lc, nkv), kv_spec(lc, 2 * nkv)],
        out_specs=pl.BlockSpec((1, tq, 2 * LANES), lambda bi, i, j: (bi, i, j)),
        out_shape=jax.ShapeDtypeStruct((b, s, nq), BF16),
        compiler_params=_cparams(("parallel", "arbitrary", "arbitrary"), 48),
        name="gqa_attention",
    )(q, k, v, kc, vc)


def _na_bias_table(rpb):
    qc = np.arange(GRID_W)[:, None]
    kc = np.arange(GRID_W)[None, :]
    c0 = np.clip(qc - NA_COLS // 2, 0, GRID_W - NA_COLS)
    in_win = (kc >= c0) & (kc < c0 + NA_COLS)
    dc = np.clip(kc - qc, -(NA_COLS - 1), NA_COLS - 1) + NA_COLS - 1
    t = jnp.where(in_win, rpb[:, :, dc] * LOG2E, NEG_INF)
    t = jnp.concatenate([t, jnp.full_like(t[:, :1], NEG_INF)], axis=1)
    return jnp.concatenate([t, t], axis=-1)


def _gqa_layout(n_heads, n_kv):
    half = HDIM // 2
    q_cols = [(n_kv * s + jj) * HDIM + p * half + i
              for jj in range(n_heads // n_kv) for p in range(2) for s in range(n_kv) for i in range(half)]
    k_cols = [s * HDIM + p * half + i for p in range(2) for s in range(n_kv) for i in range(half)]
    y_rows = [(n_kv * s + jj) * HDIM + c
              for jj in range(n_heads // n_kv) for s in range(n_kv) for c in range(HDIM)]
    return np.array(q_cols), np.array(k_cols), np.array(y_rows)


def _rope_tables(n_tokens):
    t = jnp.arange(n_tokens)
    quarter = HDIM // 4
    inv = ROPE_THETA ** (-jnp.arange(quarter, dtype=F32) / quarter)
    rows = (t // GRID_W).astype(F32)[:, None] * inv
    cols = (t % GRID_W).astype(F32)[:, None] * inv
    ang = jnp.tile(jnp.concatenate([rows, cols], axis=-1), (1, LANES // (HDIM // 2)))
    return jnp.cos(ang), jnp.sin(ang)


def kernel(x, c, ctx, c_ctx, w_ada, b_ada, norm_g, w_in_ab, a_w_s, a_b_s, a_v_g, b_rpb, w_out_ab,
           w_qkv_c, c_q_g, c_k_g, w_out_c, w_up, conv_w, conv_b, w_down):
    bsz, seq, d = x.shape
    lc = ctx.shape[1]
    depth = w_ada.shape[0]
    aw = a_v_g.shape[-1]
    n_heads_b = b_rpb.shape[1]
    bw = n_heads_b * HDIM
    n_kv = 4
    n_heads_c = w_out_c.shape[1] // HDIM
    assert seq % (NA_QROWS * GRID_W) == 0 and seq // GRID_W >= NA_KROWS
    assert aw // A_GROUPS == A_CHUNK and w_up.shape[-1] // 2 % FFN_CW == 0

    n_rows = 8 * ((bsz + 1 + 7) // 8)
    cv = jnp.zeros((n_rows, d), F32).at[:bsz].set(c).at[bsz].set(c_ctx)
    mods = _ada(cv, w_ada, b_ada)

    def mod_vectors(layer):
        m = mods[layer].reshape(n_rows, 6, d)
        lat = [m[:bsz, j][:, None, :] for j in range(6)]
        cx = [jnp.broadcast_to(m[bsz, j][None, None, :], (bsz, 1, d)) for j in range(6)]
        return lat, cx

    cos_l, sin_l = _rope_tables(seq)
    cos_c, sin_c = jnp.ones((lc, LANES), F32), jnp.zeros((lc, LANES), F32)

    h, hc = x, ctx
    for layer in range(depth):
        with_ctx = layer < depth - 1
        lat, cx = mod_vectors(layer)
        g_pre_m, g_post_m, g_pre_f, g_post_f = [norm_g[layer, j][None, :] for j in range(4)]
        if layer % 2 == 0:
            e = layer // 2
            w_in = w_in_ab[e].astype(BF16)
            w_s = a_w_s[e].astype(BF16)
            b_s_b = jnp.broadcast_to(a_b_s[e][:, :, None], a_w_s[e].shape[:2] + (aw // A_GROUPS,))
            g_v = a_v_g[e][None, :]
            w_out = w_out_ab[e].astype(BF16)
            ya, q, k, v = _inproj_ab(h, lat[0], lat[1], g_pre_m, w_in, w_s, b_s_b, g_v, aw, bw)
            yca, qc, kc, vc = _inproj_ab(hc, cx[0], cx[1], g_pre_m, w_in, w_s, b_s_b, g_v, aw, bw)
            yb = _na_attention(q, k, v, kc, vc, _na_bias_table(b_rpb[e]))
            h = _outproj([ya, yb], [w_out[:aw], w_out[aw:]], h, lat[2], g_post_m)
            if with_ctx:
                ycb = _ctx_attention(qc, kc, vc)
                hc = _outproj([yca, ycb], [w_out[:aw], w_out[aw:]], hc, cx[2], g_post_m)
        else:
            o = layer // 2
            q_cols, k_cols, y_rows = _gqa_layout(n_heads_c, n_kv)
            nq = n_heads_c * HDIM
            nkv = n_kv * HDIM
            wqkv = w_qkv_c[o]
            wq = wqkv[:, :nq][:, q_cols].astype(BF16)
            wk = wqkv[:, nq:nq + nkv][:, k_cols].astype(BF16)
            wv = wqkv[:, nq + nkv:].astype(BF16)
            half = HDIM // 2
            reps = LANES // half
            gq = (jnp.stack([jnp.tile(c_q_g[o][:half], reps), jnp.tile(c_q_g[o][half:], reps)])
                  * (HDIM ** -0.5 * LOG2E))
            gk = jnp.stack([jnp.tile(c_k_g[o][:half], reps), jnp.tile(c_k_g[o][half:], reps)])
            seg = np.arange(LANES) // half
            bd = jnp.asarray(seg[:, None] == seg[None, :], BF16)
            w_out = w_out_c[o][y_rows].astype(BF16)
            q, k, v = _inproj_c(h, lat[0], lat[1], g_pre_m, wq, wk, wv, gq, gk, bd, cos_l, sin_l, True)
            kc, vc = _inproj_c(hc, cx[0], cx[1], g_pre_m, wq, wk, wv, gq, gk, bd, cos_c, sin_c, False)
            y = _gqa_attention(q, k, v, kc, vc)
            h = _outproj([y], [w_out], h, lat[2], g_post_m)
            if with_ctx:
                raise NotImplementedError("context update after a grouped-query layer")
        wts = _ffn_weights(w_up[layer], conv_w[layer], conv_b[layer], w_down[layer])
        h = _ffn(h, lat[3], lat[4], lat[5], g_pre_f, g_post_f, wts)
        if with_ctx:
            hc = _ffn(hc, cx[3], cx[4], cx[5], g_pre_f, g_post_f, wts)
    return h
```

```python
import functools

import numpy as np
import jax
import jax.numpy as jnp
from jax import lax
from jax.experimental import pallas as pl
from jax.experimental.pallas import tpu as pltpu

F32 = jnp.float32
BF16 = jnp.bfloat16

EPS = 1e-6
NEG_INF = -1e30
GRID_W = 64
ROPE_THETA = 10000.0
LOG2E = 1.4426950408889634
A_CHUNK = 128
A_GROUPS = 4
NA_ROWS = 8
NA_COLS = 16
HDIM = 64
LANES = 128
NA_QROWS = 4
NA_KROWS = 12
SUBLANES = 8
FFN_CW = 256
V7X_VMEM_BYTES = 64 * 1024 * 1024


def _cparams(sem, vmem_mb):
    assert vmem_mb * 1024 * 1024 < V7X_VMEM_BYTES
    return pltpu.CompilerParams(dimension_semantics=sem, vmem_limit_bytes=vmem_mb * 1024 * 1024)


def _tile(n, pref):
    return pref if n % pref == 0 else n


def _vec_spec(vec):
    table, row = vec
    return pl.BlockSpec((1, 1, table.shape[-1]), lambda bi, *_: (row(bi), 0, 0))


def _rms(x, g):
    return x * lax.rsqrt(jnp.mean(x * x, axis=-1, keepdims=True) + EPS) * g


def _modulate(x, g, shift, scale):
    return _rms(x, g) * (1.0 + scale) + shift


def _gelu(x):
    return 0.5 * x * (1.0 + jnp.tanh(0.7978845608028654 * (x + 0.044715 * (x * x * x))))


def _silu(x):
    return x * jax.nn.sigmoid(x)


def _dot(a, b):
    return jnp.dot(a, b, preferred_element_type=F32)


def _dot_t(a, b):
    return lax.dot_general(a, b, (((1,), (1,)), ((), ())), preferred_element_type=F32)


def _ada_body(c_ref, w_ref, b_ref, o_ref):
    s = _silu(c_ref[...])
    o_ref[0] = jnp.dot(s, w_ref[0], preferred_element_type=F32,
                       precision=lax.Precision.HIGHEST) + b_ref[0]


def _ada(cv, w_ada, b_ada):
    depth, d, n = w_ada.shape
    rows = cv.shape[0]
    tn = _tile(n, 1536)
    return pl.pallas_call(
        _ada_body,
        grid=(depth, n // tn),
        in_specs=[pl.BlockSpec((rows, d), lambda l, j: (0, 0)),
                  pl.BlockSpec((1, d, tn), lambda l, j: (l, 0, j)),
                  pl.BlockSpec((1, 1, tn), lambda l, j: (l, 0, j))],
        out_specs=pl.BlockSpec((1, rows, tn), lambda l, j: (l, 0, j)),
        out_shape=jax.ShapeDtypeStruct((depth, rows, n), F32),
        compiler_params=_cparams(("parallel", "parallel"), 40),
        name="ada",
    )(cv, w_ada, b_ada.reshape(depth, 1, n))


def _store_ones_padded(v_ref, rows, vals):
    lo = lax.broadcasted_iota(jnp.int32, (vals.shape[0], LANES), 1) < HDIM
    for t in range(vals.shape[1] // LANES):
        vt = vals[:, t * LANES:(t + 1) * LANES]
        v_ref[0, rows, (2 * t) * LANES:(2 * t + 1) * LANES] = jnp.where(lo, vt, 1.0).astype(BF16)
        v_ref[0, rows, (2 * t + 1) * LANES:(2 * t + 2) * LANES] = jnp.where(lo, 1.0, vt).astype(BF16)


def _inproj_ab_body(h_ref, sh_ref, sc_ref, g_ref, w_ref, ws_ref, bs_ref, gv_ref,
                    ya_ref, q_ref, k_ref, v_ref, *, sub, n_sub, aw):
    bw = q_ref.shape[-1]
    gd = aw // A_GROUPS
    bounds = [0, aw, 2 * aw, 2 * aw + bw, 2 * aw + 2 * bw, 2 * aw + 3 * bw]
    projs = []
    for t in range(n_sub):
        xm = _modulate(h_ref[0, t * sub:(t + 1) * sub], g_ref[0], sh_ref[0], sc_ref[0]).astype(BF16)
        projs.append([_dot(xm, w_ref[:, c0:c1]) for c0, c1 in zip(bounds[:-1], bounds[1:])])
    for t in range(n_sub):
        rows = slice(t * sub, (t + 1) * sub)
        tu, tva, tq, tk, tv = projs[t]
        q_ref[0, rows] = (tq * (HDIM ** -0.5 * LOG2E)).astype(BF16)
        k_ref[0, rows] = tk.astype(BF16)
        _store_ones_padded(v_ref, rows, tv)
        u = _gelu(tu)
        va = _gelu(tva)
        mu = jnp.mean(va, axis=-1, keepdims=True)
        d = va - mu
        var = jnp.mean(d * d, axis=-1, keepdims=True)
        vn = (d * lax.rsqrt(var + EPS) * gv_ref[...]).astype(BF16)
        for n in range(sub // A_CHUNK):
            r0, r1 = n * A_CHUNK, (n + 1) * A_CHUNK
            for g in range(A_GROUPS):
                c0, c1 = g * gd, (g + 1) * gd
                s = _dot(ws_ref[g], vn[r0:r1, c0:c1]) + bs_ref[g]
                ya_ref[0, t * sub + r0:t * sub + r1, c0:c1] = (u[r0:r1, c0:c1] * s).astype(BF16)


def _inproj_ab(h, sh, sc, g_pre, w_in, w_s, b_s_b, g_v, aw, bw):
    b, l, d = h.shape
    sub = _tile(l, 512)
    n_sub = 2 if l % (2 * sub) == 0 else 1
    tm = sub * n_sub
    n_in = w_in.shape[1]
    row = lambda bi, i: (bi, i, 0)
    vec = lambda bi, i: (bi, 0, 0)
    c2 = lambda bi, i: (0, 0)
    c3 = lambda bi, i: (0, 0, 0)
    body = functools.partial(_inproj_ab_body, sub=sub, n_sub=n_sub, aw=aw)
    widths = (aw, bw, bw, 2 * bw)
    return pl.pallas_call(
        body,
        grid=(b, l // tm),
        in_specs=[pl.BlockSpec((1, tm, d), row), _vec_spec(sh), _vec_spec(sc), _vec_spec(g_pre),
                  pl.BlockSpec((d, n_in), c2),
                  pl.BlockSpec(w_s.shape, c3), pl.BlockSpec(b_s_b.shape, c3),
                  pl.BlockSpec((1, aw), c2)],
        out_specs=[pl.BlockSpec((1, tm, w), row) for w in widths],
        out_shape=[jax.ShapeDtypeStruct((b, l, w), BF16) for w in widths],
        compiler_params=_cparams(("parallel", "parallel"), 48),
        name="inproj_ab",
    )(h, sh[0], sc[0], g_pre[0], w_in, w_s, b_s_b, g_v)


def _softmax_pv(scores, values):
    m = scores[0].max(axis=-1, keepdims=True)
    for s in scores[1:]:
        m = jnp.maximum(m, s.max(axis=-1, keepdims=True))
    o = _dot(jnp.exp2(scores[0] - m).astype(BF16), values[0])
    for s, v in zip(scores[1:], values[1:]):
        o = o + _dot(jnp.exp2(s - m).astype(BF16), v)
    return o


def _normalised_pair(o_even, o_odd):
    lo = lax.broadcasted_iota(jnp.int32, o_even.shape, 1) < HDIM
    return jnp.where(lo, o_even / pltpu.roll(o_even, HDIM, axis=1), o_odd / pltpu.roll(o_odd, HDIM, axis=1))


def _na_body(q_ref, k_ref, v_ref, kc_ref, vc_ref, t_ref, o_ref, *, rows):
    rb = pl.program_id(1)
    tq = NA_QROWS * GRID_W
    nk = NA_KROWS * GRID_W
    start = jnp.clip(NA_QROWS * rb - NA_ROWS // 2, 0, rows - NA_KROWS)
    koff = pl.multiple_of(start * GRID_W, NA_QROWS * GRID_W)
    lo_q = lax.broadcasted_iota(jnp.int32, (tq, LANES), 1) < HDIM
    lo_b = lax.broadcasted_iota(jnp.int32, (GRID_W, LANES), 1) < GRID_W
    idx = []
    for ri in range(NA_QROWS):
        r = NA_QROWS * rb + ri
        r0 = jnp.clip(r - NA_ROWS // 2, 0, rows - NA_ROWS)
        row = []
        for j in range(NA_KROWS):
            kr = start + j
            valid = jnp.logical_and(kr >= r0, kr < r0 + NA_ROWS)
            row.append(jnp.where(valid, kr - r + NA_ROWS - 1, 2 * NA_ROWS - 1))
        idx.append(row)
    for p in range(q_ref.shape[-1] // LANES):
        cp = slice(p * LANES, (p + 1) * LANES)
        q = q_ref[0, :, cp]
        zero = jnp.zeros_like(q)
        lhs = jnp.concatenate([jnp.where(lo_q, q, zero), jnp.where(lo_q, zero, q)], axis=0)
        bias = jnp.concatenate([
            jnp.concatenate([
                jnp.where(lo_b, t_ref[2 * p + a, idx[ri][2 * m]], t_ref[2 * p + a, idx[ri][2 * m + 1]])
                for m in range(NA_KROWS // 2)], axis=1)
            for a in range(2) for ri in range(NA_QROWS)], axis=0)
        s_loc = _dot_t(lhs, k_ref[0, pl.ds(koff, nk), cp]) + bias
        s_ctx = _dot_t(lhs, kc_ref[0, :, cp])
        m = jnp.maximum(s_loc.max(axis=-1, keepdims=True), s_ctx.max(axis=-1, keepdims=True))
        p_loc = jnp.exp2(s_loc - m).astype(BF16)
        p_ctx = jnp.exp2(s_ctx - m).astype(BF16)
        outs = []
        for a in range(2):
            ch = slice((2 * p + a) * LANES, (2 * p + a + 1) * LANES)
            rs = slice(a * tq, (a + 1) * tq)
            outs.append(_dot(p_loc[rs], v_ref[0, pl.ds(koff, nk), ch]) + _dot(p_ctx[rs], vc_ref[0, :, ch]))
        o_ref[0, :, cp] = _normalised_pair(outs[0], outs[1]).astype(BF16)


def _na_attention(q, k, v, kc, vc, tdup):
    b, s, w = q.shape
    lc = kc.shape[1]
    rows = s // GRID_W
    tq = NA_QROWS * GRID_W
    body = functools.partial(_na_body, rows=rows)
    whole = lambda n, wd: pl.BlockSpec((1, n, wd), lambda bi, r: (bi, 0, 0))
    return pl.pallas_call(
        body,
        grid=(b, rows // NA_QROWS),
        in_specs=[pl.BlockSpec((1, tq, w), lambda bi, r: (bi, r, 0)),
                  whole(s, w), whole(s, 2 * w), whole(lc, w), whole(lc, 2 * w),
                  pl.BlockSpec(tdup.shape, lambda bi, r: (0, 0, 0, 0))],
        out_specs=pl.BlockSpec((1, tq, w), lambda bi, r: (bi, r, 0)),
        out_shape=jax.ShapeDtypeStruct((b, s, w), BF16),
        compiler_params=_cparams(("parallel", "arbitrary"), 48),
        name="na_attention",
    )(q, k, v, kc, vc, tdup)


def _ctx_attn_body(q_ref, k_ref, v_ref, o_ref):
    q = q_ref[0]
    k = k_ref[0]
    lo = lax.broadcasted_iota(jnp.int32, q.shape, 1) < HDIM
    outs = []
    for a in range(2):
        qa = jnp.where(lo if a == 0 else jnp.logical_not(lo), q, jnp.zeros_like(q))
        outs.append(_softmax_pv([_dot_t(qa, k)], [v_ref[0, :, a * LANES:(a + 1) * LANES]]))
    o_ref[0] = _normalised_pair(outs[0], outs[1]).astype(BF16)


def _ctx_attention(q, k, v):
    b, l, w = q.shape
    spec = pl.BlockSpec((1, l, LANES), lambda bi, p: (bi, 0, p))
    return pl.pallas_call(
        _ctx_attn_body,
        grid=(b, w // LANES),
        in_specs=[spec, spec, pl.BlockSpec((1, l, 2 * LANES), lambda bi, p: (bi, 0, p))],
        out_specs=spec,
        out_shape=jax.ShapeDtypeStruct((b, l, w), BF16),
        compiler_params=_cparams(("parallel", "parallel"), 32),
        name="ctx_attention",
    )(q, k, v)


def _outproj_body(*refs, n_parts):
    y_refs = refs[:n_parts]
    w_refs = refs[n_parts:2 * n_parts]
    h_ref, gt_ref, g_ref, o_ref = refs[2 * n_parts:]
    y = _dot(y_refs[0][0], w_refs[0][...])
    for yr, wr in zip(y_refs[1:], w_refs[1:]):
        y = y + _dot(yr[0], wr[...])
    o_ref[0] = h_ref[0] + gt_ref[0] * _rms(y, g_ref[0])


def _outproj(ys, ws, h, gt, g_post):
    b, l, d = h.shape
    tm = _tile(l, 1024)
    row = lambda bi, i: (bi, i, 0)
    vec = lambda bi, i: (bi, 0, 0)
    c2 = lambda bi, i: (0, 0)
    body = functools.partial(_outproj_body, n_parts=len(ys))
    return pl.pallas_call(
        body,
        grid=(b, l // tm),
        in_specs=([pl.BlockSpec((1, tm, y.shape[-1]), row) for y in ys]
                  + [pl.BlockSpec(w.shape, c2) for w in ws]
                  + [pl.BlockSpec((1, tm, d), row), _vec_spec(gt), _vec_spec(g_post)]),
        out_specs=pl.BlockSpec((1, tm, d), row),
        out_shape=jax.ShapeDtypeStruct((b, l, d), F32),
        compiler_params=_cparams(("parallel", "parallel"), 40),
        name="outproj",
    )(*ys, *ws, h, gt[0], g_post[0])


def _ffn_body(xp_ref, x_ref, xn_ref, sh_ref, sc_ref, gt_ref, gpre_ref, gpost_ref,
              wa_ref, wg_ref, cwa_ref, cwg_ref, cba_ref, cbg_ref, wd_ref,
              o_ref, xs_ref, p3_ref, act_ref, acc_ref, *, tm, n_tiles, n_chunks, groups):
    i = pl.program_id(1)
    g = gpre_ref[0]
    sh = sh_ref[0]
    sc = sc_ref[0]
    nv = tm // SUBLANES
    nl = x_ref.shape[-1] // LANES
    xm = _modulate(x_ref[0], g, sh, sc)
    for j in range(nl):
        for s in range(SUBLANES):
            p3_ref[j, pl.ds(s, nv, stride=SUBLANES), :] = xm[s * nv:(s + 1) * nv, j * LANES:(j + 1) * LANES]
    xs_ref[0:tm] = jnp.concatenate([p3_ref[j] for j in range(nl)], axis=1).astype(BF16)
    rid = lax.broadcasted_iota(jnp.int32, (2 * SUBLANES, 1), 0)
    keep = jnp.where(rid == 0, (i > 0).astype(F32), jnp.where(rid == 1, (i < n_tiles - 1).astype(F32), 0.0))
    halo = jnp.concatenate([xp_ref[0, SUBLANES - 1:SUBLANES], xn_ref[0, 0:1],
                            jnp.zeros((2 * SUBLANES - 2, x_ref.shape[-1]), F32)], axis=0)
    xs_ref[tm:tm + 2 * SUBLANES] = (_modulate(halo, g, sh, sc) * keep).astype(BF16)

    srow = lax.broadcasted_iota(jnp.int32, (SUBLANES, FFN_CW), 0)

    def conv(hh, cw, cb):
        hm = hh[0:tm]
        first = jnp.where(srow == 0, hh[tm:tm + 1], pltpu.roll(hm[tm - SUBLANES:tm], 1, axis=0))
        last = jnp.where(srow == SUBLANES - 1, hh[tm + 1:tm + 2], pltpu.roll(hm[0:SUBLANES], SUBLANES - 1, axis=0))
        prev = jnp.concatenate([first, hm[0:tm - SUBLANES]], axis=0)
        nxt = jnp.concatenate([hm[SUBLANES:tm], last], axis=0)
        return prev * cw[0:1] + hm * cw[1:2] + nxt * cw[2:3] + cb

    def up(c):
        xs = xs_ref[...]
        cc = slice(c * FFN_CW, (c + 1) * FFN_CW)
        return _dot(xs, wa_ref[:, cc]), _dot(xs, wg_ref[:, cc])

    pending = up(0)
    k0 = 0
    for c in range(n_chunks):
        ha, hg = pending
        if c + 1 < n_chunks:
            pending = up(c + 1)
        cc = slice(c * FFN_CW, (c + 1) * FFN_CW)
        a = conv(ha, cwa_ref[:, cc], cba_ref[:, cc])
        gg = conv(hg, cwg_ref[:, cc], cbg_ref[:, cc])
        act_ref[:, cc] = (_silu(gg) * a).astype(BF16)
        if c + 1 in groups:
            k1 = (c + 1) * FFN_CW
            part = _dot(act_ref[:, k0:k1], wd_ref[k0:k1, :])
            if k0 == 0:
                acc_ref[...] = part
            else:
                acc_ref[...] += part
            k0 = k1
    f = gt_ref[0] * _rms(acc_ref[...], gpost_ref[0])
    for j in range(nl):
        p3_ref[j] = f[:, j * LANES:(j + 1) * LANES]
    for s in range(SUBLANES):
        fs = jnp.concatenate([p3_ref[j, pl.ds(s, nv, stride=SUBLANES), :] for j in range(nl)], axis=1)
        o_ref[0, s * nv:(s + 1) * nv, :] = x_ref[0, s * nv:(s + 1) * nv, :] + fs


def _ffn(h, sh, sc, gt, g_pre, g_post, wts):
    wa, wg, cwa, cwg, cba, cbg, wd = wts
    b, l, d = h.shape
    tm = _tile(l, 512)
    n_tiles = l // tm
    n_chunks = wa.shape[1] // FFN_CW
    ff = wd.shape[0]
    groups = (4, 8, n_chunks)
    hb = tm // SUBLANES
    last_hb = l // SUBLANES - 1
    row = lambda bi, i: (bi, i, 0)
    vec = lambda bi, i: (bi, 0, 0)
    c2 = lambda bi, i: (0, 0)
    whole = lambda a: pl.BlockSpec(a.shape, lambda bi, i: (0,) * a.ndim, pipeline_mode=pl.Buffered(1))
    body = functools.partial(_ffn_body, tm=tm, n_tiles=n_tiles, n_chunks=n_chunks, groups=groups)
    return pl.pallas_call(
        body,
        grid=(b, n_tiles),
        in_specs=[pl.BlockSpec((1, SUBLANES, d), lambda bi, i: (bi, jnp.maximum(i * hb - 1, 0), 0)),
                  pl.BlockSpec((1, tm, d), row),
                  pl.BlockSpec((1, SUBLANES, d), lambda bi, i: (bi, jnp.minimum((i + 1) * hb, last_hb), 0)),
                  _vec_spec(sh), _vec_spec(sc), _vec_spec(gt), _vec_spec(g_pre), _vec_spec(g_post),
                  whole(wa), whole(wg), whole(cwa), whole(cwg), whole(cba), whole(cbg), whole(wd)],
        out_specs=pl.BlockSpec((1, tm, d), row),
        out_shape=jax.ShapeDtypeStruct((b, l, d), F32),
        scratch_shapes=[pltpu.VMEM((tm + 2 * SUBLANES, d), BF16), pltpu.VMEM((d // LANES, tm, LANES), F32),
                        pltpu.VMEM((tm, ff), BF16), pltpu.VMEM((tm, d), F32)],
        compiler_params=_cparams(("parallel", "parallel"), 56),
        name="conv_ffn",
    )(h, h, h, sh[0], sc[0], gt[0], g_pre[0], g_post[0], wa, wg, cwa, cwg, cba, cbg, wd)


def _ffn_weights(w_up, conv_w, conv_b, w_down):
    ff = w_up.shape[1] // 2
    return (w_up[:, :ff].astype(BF16), w_up[:, ff:].astype(BF16), conv_w[:, :ff], conv_w[:, ff:],
            conv_b[None, :ff], conv_b[None, ff:], w_down.astype(BF16))


def _inproj_c_body(h_ref, sh_ref, sc_ref, g_ref, wq_ref, wk_ref, wv_ref, gq_ref, gk_ref, bd_ref,
                   cos_ref, sin_ref, *out_refs, n_qtiles, sub, n_sub):
    bd = bd_ref[...]
    if n_qtiles:
        q_ref, k_ref, v_ref = out_refs
    else:
        k_ref, v_ref = out_refs
    projs = []
    for t in range(n_sub):
        xm = _modulate(h_ref[0, t * sub:(t + 1) * sub], g_ref[0], sh_ref[0], sc_ref[0]).astype(BF16)
        tq = _dot(xm, wq_ref[...]) if n_qtiles else None
        projs.append((tq, _dot(xm, wk_ref[...]), _dot(xm, wv_ref[...])))
    for t in range(n_sub):
        rows = slice(t * sub, (t + 1) * sub)
        tq, tk, tv = projs[t]
        _store_ones_padded(v_ref, rows, tv)
        tiles = [(tq[:, 2 * LANES * j:2 * LANES * j + LANES], tq[:, 2 * LANES * j + LANES:2 * LANES * (j + 1)])
                 for j in range(n_qtiles)] + [(tk[:, :LANES], tk[:, LANES:])]
        sq = jnp.concatenate([(t1 * t1 + t2 * t2).astype(BF16) for t1, t2 in tiles], axis=0)
        ss = _dot(sq, bd)
        cs = cos_ref[rows]
        sn = sin_ref[rows]
        for j, (t1, t2) in enumerate(tiles):
            is_q = j < n_qtiles
            o_ref = q_ref if is_q else k_ref
            gains = gq_ref[...] if is_q else gk_ref[...]
            c0 = 2 * LANES * j if is_q else 0
            r = lax.rsqrt(ss[j * sub:(j + 1) * sub] * (1.0 / HDIM) + EPS)
            a1 = t1 * r * gains[0:1]
            a2 = t2 * r * gains[1:2]
            o_ref[0, rows, c0:c0 + LANES] = (a1 * cs - a2 * sn).astype(BF16)
            o_ref[0, rows, c0 + LANES:c0 + 2 * LANES] = (a1 * sn + a2 * cs).astype(BF16)


def _inproj_c(h, sh, sc, g_pre, wq, wk, wv, gq, gk, bd, cos, sin, with_q):
    b, l, d = h.shape
    sub = _tile(l, 512)
    n_sub = 2 if l % (2 * sub) == 0 else 1
    tm = sub * n_sub
    row = lambda bi, i: (bi, i, 0)
    vec = lambda bi, i: (bi, 0, 0)
    c2 = lambda bi, i: (0, 0)
    nq, nkv = wq.shape[1], wk.shape[1]
    n_qtiles = nq // (2 * LANES) if with_q else 0
    body = functools.partial(_inproj_c_body, n_qtiles=n_qtiles, sub=sub, n_sub=n_sub)
    out_specs = [pl.BlockSpec((1, tm, nkv), row), pl.BlockSpec((1, tm, 2 * nkv), row)]
    out_shape = [jax.ShapeDtypeStruct((b, l, nkv), BF16), jax.ShapeDtypeStruct((b, l, 2 * nkv), BF16)]
    if with_q:
        out_specs = [pl.BlockSpec((1, tm, nq), row)] + out_specs
        out_shape = [jax.ShapeDtypeStruct((b, l, nq), BF16)] + out_shape
    return pl.pallas_call(
        body,
        grid=(b, l // tm),
        in_specs=[pl.BlockSpec((1, tm, d), row), _vec_spec(sh), _vec_spec(sc), _vec_spec(g_pre),
                  pl.BlockSpec(wq.shape, c2), pl.BlockSpec(wk.shape, c2), pl.BlockSpec(wv.shape, c2),
                  pl.BlockSpec(gq.shape, c2), pl.BlockSpec(gk.shape, c2), pl.BlockSpec(bd.shape, c2),
                  pl.BlockSpec((tm, LANES), lambda bi, i: (i, 0)),
                  pl.BlockSpec((tm, LANES), lambda bi, i: (i, 0))],
        out_specs=out_specs,
        out_shape=out_shape,
        compiler_params=_cparams(("parallel", "parallel"), 40),
        name="inproj_c",
    )(h, sh[0], sc[0], g_pre[0], wq, wk, wv, gq, gk, bd, cos, sin)


def _gqa_body(q_ref, k_ref, v_ref, kc_ref, vc_ref, o_ref, *, tq, n_sub):
    k = k_ref[0]
    kc = kc_ref[0]
    seg = (lax.broadcasted_iota(jnp.int32, (tq, 2 * LANES), 1) % LANES) // (HDIM // 2)
    lo = lax.broadcasted_iota(jnp.int32, (tq, LANES), 1) < HDIM
    def scores(t):
        q = q_ref[0, t * tq:(t + 1) * tq]
        zero = jnp.zeros_like(q)
        lhs = jnp.concatenate([jnp.where(seg == s, q, zero) for s in range(4)], axis=0)
        return _dot_t(lhs, k), _dot_t(lhs, kc)

    pending = scores(0)
    for t in range(n_sub):
        s_lat, s_ctx = pending
        if t + 1 < n_sub:
            pending = scores(t + 1)
        m = jnp.maximum(s_lat.max(axis=-1, keepdims=True), s_ctx.max(axis=-1, keepdims=True))
        p_lat = jnp.exp2(s_lat - m).astype(BF16)
        p_ctx = jnp.exp2(s_ctx - m).astype(BF16)
        outs = []
        for s in range(4):
            rs = slice(s * tq, (s + 1) * tq)
            cs = slice(s * LANES, (s + 1) * LANES)
            outs.append(_dot(p_lat[rs], v_ref[0, :, cs]) + _dot(p_ctx[rs], vc_ref[0, :, cs]))
        o_ref[0, t * tq:(t + 1) * tq] = jnp.concatenate(
            [_normalised_pair(outs[0], outs[1]), _normalised_pair(outs[2], outs[3])], axis=1).astype(BF16)


def _gqa_attention(q, k, v, kc, vc):
    b, s, nq = q.shape
    lc = kc.shape[1]
    nkv = k.shape[-1]
    n_sub = 4
    tq = 128 * n_sub
    body = functools.partial(_gqa_body, tq=tq // n_sub, n_sub=n_sub)
    kv_spec = lambda n, w: pl.BlockSpec((1, n, w), lambda bi, i, j: (bi, 0, 0))
    return pl.pallas_call(
        body,
        grid=(b, s // tq, nq // (2 * LANES)),
        in_specs=[pl.BlockSpec((1, tq, 2 * LANES), lambda bi, i, j: (bi, i, j)),
                  kv_spec(s, nkv), kv_spec(s, 2 * nkv), kv_spec(lc, nkv), kv_spec(lc, 2 * nkv)],
        out_specs=pl.BlockSpec((1, tq, 2 * LANES), lambda bi, i, j: (bi, i, j)),
        out_shape=jax.ShapeDtypeStruct((b, s, nq), BF16),
        compiler_params=_cparams(("parallel", "arbitrary", "arbitrary"), 48),
        name="gqa_attention",
    )(q, k, v, kc, vc)


def _na_bias_table(rpb):
    qc = np.arange(GRID_W)[:, None]
    kc = np.arange(GRID_W)[None, :]
    c0 = np.clip(qc - NA_COLS // 2, 0, GRID_W - NA_COLS)
    in_win = (kc >= c0) & (kc < c0 + NA_COLS)
    dc = np.clip(kc - qc, -(NA_COLS - 1), NA_COLS - 1) + NA_COLS - 1
    t = jnp.where(in_win, rpb[:, :, dc] * LOG2E, NEG_INF)
    t = jnp.concatenate([t, jnp.full_like(t[:, :1], NEG_INF)], axis=1)
    return jnp.concatenate([t, t], axis=-1)


def _rope_tables(n_tokens):
    t = np.arange(n_tokens)
    quarter = HDIM // 4
    inv = (ROPE_THETA ** (-np.arange(quarter, dtype=np.float32) / quarter)).astype(np.float32)
    rows = (t // GRID_W).astype(np.float32)[:, None] * inv
    cols = (t % GRID_W).astype(np.float32)[:, None] * inv
    ang = np.tile(np.concatenate([rows, cols], axis=-1), (1, LANES // (HDIM // 2)))
    return jnp.asarray(np.cos(ang), F32), jnp.asarray(np.sin(ang), F32)


def kernel(x, c, ctx, c_ctx, w_ada, b_ada, norm_g, w_in_ab, a_w_s, a_b_s, a_v_g, b_rpb, w_out_ab,
           w_qkv_c, c_q_g, c_k_g, w_out_c, w_up, conv_w, conv_b, w_down):
    bsz, seq, d = x.shape
    lc = ctx.shape[1]
    depth = w_ada.shape[0]
    aw = a_v_g.shape[-1]
    n_heads_b = b_rpb.shape[1]
    bw = n_heads_b * HDIM
    n_kv = 4
    n_heads_c = w_out_c.shape[1] // HDIM
    assert seq % (NA_QROWS * GRID_W) == 0 and seq // GRID_W >= NA_KROWS
    assert aw // A_GROUPS == A_CHUNK and w_up.shape[-1] // 2 % FFN_CW == 0

    n_rows = 8 * ((bsz + 1 + 7) // 8)
    cv = jnp.concatenate([c, c_ctx[None], jnp.zeros((n_rows - bsz - 1, d), F32)], axis=0)
    mods = _ada(cv, w_ada, b_ada).reshape(depth * n_rows * 6, 1, d)
    gains = norm_g.reshape(depth * 4, 1, d)

    def mod_vec(layer, j, is_ctx):
        base = layer * n_rows * 6 + j
        return mods, ((lambda bi: base + bsz * 6) if is_ctx else (lambda bi: base + bi * 6))

    def gain_vec(layer, j):
        return gains, (lambda bi: layer * 4 + j)

    cos_l, sin_l = _rope_tables(seq)
    cos_c, sin_c = jnp.ones((lc, LANES), F32), jnp.zeros((lc, LANES), F32)

    h, hc = x, ctx
    for layer in range(depth):
        with_ctx = layer < depth - 1
        lat = [mod_vec(layer, j, False) for j in range(6)]
        cx = [mod_vec(layer, j, True) for j in range(6)]
        g_pre_m, g_post_m, g_pre_f, g_post_f = [gain_vec(layer, j) for j in range(4)]
        if layer % 2 == 0:
            e = layer // 2
            w_in = w_in_ab[e].astype(BF16)
            w_s = a_w_s[e].astype(BF16)
            b_s_b = jnp.broadcast_to(a_b_s[e][:, :, None], a_w_s[e].shape[:2] + (aw // A_GROUPS,))
            g_v = a_v_g[e][None, :]
            w_out = w_out_ab[e].astype(BF16)
            ya, q, k, v = _inproj_ab(h, lat[0], lat[1], g_pre_m, w_in, w_s, b_s_b, g_v, aw, bw)
            yca, qc, kc, vc = _inproj_ab(hc, cx[0], cx[1], g_pre_m, w_in, w_s, b_s_b, g_v, aw, bw)
            yb = _na_attention(q, k, v, kc, vc, _na_bias_table(b_rpb[e]))
            h = _outproj([ya, yb], [w_out[:aw], w_out[aw:]], h, lat[2], g_post_m)
            if with_ctx:
                ycb = _ctx_attention(qc, kc, vc)
                hc = _outproj([yca, ycb], [w_out[:aw], w_out[aw:]], hc, cx[2], g_post_m)
        else:
            o = layer // 2
            nq = n_heads_c * HDIM
            nkv = n_kv * HDIM
            per = n_heads_c // n_kv
            half = HDIM // 2
            wqkv = w_qkv_c[o].astype(BF16)
            wq = wqkv[:, :nq].reshape(d, n_kv, per, 2, half).transpose(0, 2, 3, 1, 4).reshape(d, nq)
            wk = wqkv[:, nq:nq + nkv].reshape(d, n_kv, 2, half).transpose(0, 2, 1, 3).reshape(d, nkv)
            wv = wqkv[:, nq + nkv:]
            reps = LANES // half
            gq = (jnp.stack([jnp.tile(c_q_g[o][:half], reps), jnp.tile(c_q_g[o][half:], reps)])
                  * (HDIM ** -0.5 * LOG2E))
            gk = jnp.stack([jnp.tile(c_k_g[o][:half], reps), jnp.tile(c_k_g[o][half:], reps)])
            seg = np.arange(LANES) // half
            bd = jnp.asarray(seg[:, None] == seg[None, :], BF16)
            w_out = w_out_c[o].astype(BF16).reshape(n_kv, per, HDIM, d).transpose(1, 0, 2, 3).reshape(nq, d)
            q, k, v = _inproj_c(h, lat[0], lat[1], g_pre_m, wq, wk, wv, gq, gk, bd, cos_l, sin_l, True)
            kc, vc = _inproj_c(hc, cx[0], cx[1], g_pre_m, wq, wk, wv, gq, gk, bd, cos_c, sin_c, False)
            y = _gqa_attention(q, k, v, kc, vc)
            h = _outproj([y], [w_out], h, lat[2], g_post_m)
            if with_ctx:
                raise NotImplementedError("context update after a grouped-query layer")
        wts = _ffn_weights(w_up[layer], conv_w[layer], conv_b[layer], w_down[layer])
        h = _ffn(h, lat[3], lat[4], lat[5], g_pre_f, g_post_f, wts)
        if with_ctx:
            hc = _ffn(hc, cx[3], cx[4], cx[5], g_pre_f, g_post_f, wts)
    return h
```

```python
import functools

import numpy as np
import jax
import jax.numpy as jnp
from jax import lax
from jax.experimental import pallas as pl
from jax.experimental.pallas import tpu as pltpu

F32 = jnp.float32
BF16 = jnp.bfloat16

EPS = 1e-6
NEG_INF = -1e30
GRID_W = 64
ROPE_THETA = 10000.0
LOG2E = 1.4426950408889634
A_CHUNK = 128
A_GROUPS = 4
NA_ROWS = 8
NA_COLS = 16
HDIM = 64
LANES = 128
NA_QROWS = 4
NA_KROWS = 12
SUBLANES = 8
FFN_CW = 256
V7X_VMEM_BYTES = 64 * 1024 * 1024


def _cparams(sem, vmem_mb):
    assert vmem_mb * 1024 * 1024 < V7X_VMEM_BYTES
    return pltpu.CompilerParams(dimension_semantics=sem, vmem_limit_bytes=vmem_mb * 1024 * 1024)


def _tile(n, pref):
    return pref if n % pref == 0 else n


def _vec_spec(vec):
    table, row = vec
    return pl.BlockSpec((1, 1, table.shape[-1]), lambda bi, *_: (row(bi), 0, 0))


def _rms(x, g):
    return x * lax.rsqrt(jnp.mean(x * x, axis=-1, keepdims=True) + EPS) * g


def _modulate(x, g, shift, scale):
    return _rms(x, g) * (1.0 + scale) + shift


def _gelu(x):
    return 0.5 * x * (1.0 + jnp.tanh(0.7978845608028654 * (x + 0.044715 * (x * x * x))))


def _silu(x):
    return x * jax.nn.sigmoid(x)


def _dot(a, b):
    return jnp.dot(a, b, preferred_element_type=F32)


def _dot_t(a, b):
    return lax.dot_general(a, b, (((1,), (1,)), ((), ())), preferred_element_type=F32)


def _ada_body(c_ref, w_ref, b_ref, o_ref):
    s = _silu(c_ref[...])
    o_ref[0] = jnp.dot(s, w_ref[0], preferred_element_type=F32,
                       precision=lax.Precision.HIGHEST) + b_ref[0]


def _ada(cv, w_ada, b_ada):
    depth, d, n = w_ada.shape
    rows = cv.shape[0]
    tn = _tile(n, 1536)
    return pl.pallas_call(
        _ada_body,
        grid=(depth, n // tn),
        in_specs=[pl.BlockSpec((rows, d), lambda l, j: (0, 0)),
                  pl.BlockSpec((1, d, tn), lambda l, j: (l, 0, j)),
                  pl.BlockSpec((1, 1, tn), lambda l, j: (l, 0, j))],
        out_specs=pl.BlockSpec((1, rows, tn), lambda l, j: (l, 0, j)),
        out_shape=jax.ShapeDtypeStruct((depth, rows, n), F32),
        compiler_params=_cparams(("parallel", "parallel"), 40),
        name="ada",
    )(cv, w_ada, b_ada.reshape(depth, 1, n))


def _store_ones_padded(v_ref, rows, vals):
    lo = lax.broadcasted_iota(jnp.int32, (vals.shape[0], LANES), 1) < HDIM
    for t in range(vals.shape[1] // LANES):
        vt = vals[:, t * LANES:(t + 1) * LANES]
        v_ref[0, rows, (2 * t) * LANES:(2 * t + 1) * LANES] = jnp.where(lo, vt, 1.0).astype(BF16)
        v_ref[0, rows, (2 * t + 1) * LANES:(2 * t + 2) * LANES] = jnp.where(lo, 1.0, vt).astype(BF16)


def _inproj_ab_body(h_ref, sh_ref, sc_ref, g_ref, w_ref, ws_ref, bs_ref, gv_ref,
                    ya_ref, q_ref, k_ref, v_ref, *, sub, n_sub, aw):
    bw = q_ref.shape[-1]
    gd = aw // A_GROUPS
    bounds = [0, aw, 2 * aw, 2 * aw + bw, 2 * aw + 2 * bw, 2 * aw + 3 * bw]
    projs = []
    for t in range(n_sub):
        xm = _modulate(h_ref[0, t * sub:(t + 1) * sub], g_ref[0], sh_ref[0], sc_ref[0]).astype(BF16)
        projs.append([_dot(xm, w_ref[:, c0:c1]) for c0, c1 in zip(bounds[:-1], bounds[1:])])
    for t in range(n_sub):
        rows = slice(t * sub, (t + 1) * sub)
        tu, tva, tq, tk, tv = projs[t]
        q_ref[0, rows] = (tq * (HDIM ** -0.5 * LOG2E)).astype(BF16)
        k_ref[0, rows] = tk.astype(BF16)
        _store_ones_padded(v_ref, rows, tv)
        u = _gelu(tu)
        va = _gelu(tva)
        mu = jnp.mean(va, axis=-1, keepdims=True)
        d = va - mu
        var = jnp.mean(d * d, axis=-1, keepdims=True)
        vn = (d * lax.rsqrt(var + EPS) * gv_ref[...]).astype(BF16)
        for n in range(sub // A_CHUNK):
            r0, r1 = n * A_CHUNK, (n + 1) * A_CHUNK
            for g in range(A_GROUPS):
                c0, c1 = g * gd, (g + 1) * gd
                s = _dot(ws_ref[g], vn[r0:r1, c0:c1]) + bs_ref[g]
                ya_ref[0, t * sub + r0:t * sub + r1, c0:c1] = (u[r0:r1, c0:c1] * s).astype(BF16)


def _inproj_ab(h, sh, sc, g_pre, w_in, w_s, b_s_b, g_v, aw, bw):
    b, l, d = h.shape
    sub = _tile(l, 512)
    n_sub = 2 if l % (2 * sub) == 0 else 1
    tm = sub * n_sub
    n_in = w_in.shape[1]
    row = lambda bi, i: (bi, i, 0)
    vec = lambda bi, i: (bi, 0, 0)
    c2 = lambda bi, i: (0, 0)
    c3 = lambda bi, i: (0, 0, 0)
    body = functools.partial(_inproj_ab_body, sub=sub, n_sub=n_sub, aw=aw)
    widths = (aw, bw, bw, 2 * bw)
    return pl.pallas_call(
        body,
        grid=(b, l // tm),
        in_specs=[pl.BlockSpec((1, tm, d), row), _vec_spec(sh), _vec_spec(sc), _vec_spec(g_pre),
                  pl.BlockSpec((d, n_in), c2),
                  pl.BlockSpec(w_s.shape, c3), pl.BlockSpec(b_s_b.shape, c3),
                  pl.BlockSpec((1, aw), c2)],
        out_specs=[pl.BlockSpec((1, tm, w), row) for w in widths],
        out_shape=[jax.ShapeDtypeStruct((b, l, w), BF16) for w in widths],
        compiler_params=_cparams(("parallel", "parallel"), 48),
        name="inproj_ab",
    )(h, sh[0], sc[0], g_pre[0], w_in, w_s, b_s_b, g_v)


def _softmax_pv(scores, values):
    m = scores[0].max(axis=-1, keepdims=True)
    for s in scores[1:]:
        m = jnp.maximum(m, s.max(axis=-1, keepdims=True))
    o = _dot(jnp.exp2(scores[0] - m).astype(BF16), values[0])
    for s, v in zip(scores[1:], values[1:]):
        o = o + _dot(jnp.exp2(s - m).astype(BF16), v)
    return o


def _normalised_pair(o_even, o_odd):
    lo = lax.broadcasted_iota(jnp.int32, o_even.shape, 1) < HDIM
    return jnp.where(lo, o_even / pltpu.roll(o_even, HDIM, axis=1), o_odd / pltpu.roll(o_odd, HDIM, axis=1))


def _na_body(q_ref, k_ref, v_ref, kc_ref, vc_ref, t_ref, o_ref, *, rows):
    rb = pl.program_id(1)
    tq = NA_QROWS * GRID_W
    nk = NA_KROWS * GRID_W
    start = jnp.clip(NA_QROWS * rb - NA_ROWS // 2, 0, rows - NA_KROWS)
    koff = pl.multiple_of(start * GRID_W, NA_QROWS * GRID_W)
    lo_q = lax.broadcasted_iota(jnp.int32, (tq, LANES), 1) < HDIM
    lo_b = lax.broadcasted_iota(jnp.int32, (GRID_W, LANES), 1) < GRID_W
    idx = []
    for ri in range(NA_QROWS):
        r = NA_QROWS * rb + ri
        r0 = jnp.clip(r - NA_ROWS // 2, 0, rows - NA_ROWS)
        row = []
        for j in range(NA_KROWS):
            kr = start + j
            valid = jnp.logical_and(kr >= r0, kr < r0 + NA_ROWS)
            row.append(jnp.where(valid, kr - r + NA_ROWS - 1, 2 * NA_ROWS - 1))
        idx.append(row)
    for p in range(q_ref.shape[-1] // LANES):
        cp = slice(p * LANES, (p + 1) * LANES)
        q = q_ref[0, :, cp]
        zero = jnp.zeros_like(q)
        lhs = jnp.concatenate([jnp.where(lo_q, q, zero), jnp.where(lo_q, zero, q)], axis=0)
        bias = jnp.concatenate([
            jnp.concatenate([
                jnp.where(lo_b, t_ref[2 * p + a, idx[ri][2 * m]], t_ref[2 * p + a, idx[ri][2 * m + 1]])
                for m in range(NA_KROWS // 2)], axis=1)
            for a in range(2) for ri in range(NA_QROWS)], axis=0)
        s_loc = _dot_t(lhs, k_ref[0, pl.ds(koff, nk), cp]) + bias
        s_ctx = _dot_t(lhs, kc_ref[0, :, cp])
        m = jnp.maximum(s_loc.max(axis=-1, keepdims=True), s_ctx.max(axis=-1, keepdims=True))
        p_loc = jnp.exp2(s_loc - m).astype(BF16)
        p_ctx = jnp.exp2(s_ctx - m).astype(BF16)
        outs = []
        for a in range(2):
            ch = slice((2 * p + a) * LANES, (2 * p + a + 1) * LANES)
            rs = slice(a * tq, (a + 1) * tq)
            outs.append(_dot(p_loc[rs], v_ref[0, pl.ds(koff, nk), ch]) + _dot(p_ctx[rs], vc_ref[0, :, ch]))
        o_ref[0, :, cp] = _normalised_pair(outs[0], outs[1]).astype(BF16)


def _na_attention(q, k, v, kc, vc, tdup):
    b, s, w = q.shape
    lc = kc.shape[1]
    rows = s // GRID_W
    tq = NA_QROWS * GRID_W
    body = functools.partial(_na_body, rows=rows)
    whole = lambda n, wd: pl.BlockSpec((1, n, wd), lambda bi, r: (bi, 0, 0))
    return pl.pallas_call(
        body,
        grid=(b, rows // NA_QROWS),
        in_specs=[pl.BlockSpec((1, tq, w), lambda bi, r: (bi, r, 0)),
                  whole(s, w), whole(s, 2 * w), whole(lc, w), whole(lc, 2 * w),
                  pl.BlockSpec(tdup.shape, lambda bi, r: (0, 0, 0, 0))],
        out_specs=pl.BlockSpec((1, tq, w), lambda bi, r: (bi, r, 0)),
        out_shape=jax.ShapeDtypeStruct((b, s, w), BF16),
        compiler_params=_cparams(("parallel", "arbitrary"), 48),
        name="na_attention",
    )(q, k, v, kc, vc, tdup)


def _ctx_attn_body(q_ref, k_ref, v_ref, o_ref):
    lo = lax.broadcasted_iota(jnp.int32, (q_ref.shape[1], LANES), 1) < HDIM
    for p in range(q_ref.shape[-1] // LANES):
        cp = slice(p * LANES, (p + 1) * LANES)
        q = q_ref[0, :, cp]
        k = k_ref[0, :, cp]
        outs = []
        for a in range(2):
            ch = slice((2 * p + a) * LANES, (2 * p + a + 1) * LANES)
            qa = jnp.where(lo if a == 0 else jnp.logical_not(lo), q, jnp.zeros_like(q))
            outs.append(_softmax_pv([_dot_t(qa, k)], [v_ref[0, :, ch]]))
        o_ref[0, :, cp] = _normalised_pair(outs[0], outs[1]).astype(BF16)


def _ctx_attention(q, k, v):
    b, l, w = q.shape
    spec = pl.BlockSpec((1, l, w), lambda bi: (bi, 0, 0))
    return pl.pallas_call(
        _ctx_attn_body,
        grid=(b,),
        in_specs=[spec, spec, pl.BlockSpec((1, l, 2 * w), lambda bi: (bi, 0, 0))],
        out_specs=spec,
        out_shape=jax.ShapeDtypeStruct((b, l, w), BF16),
        compiler_params=_cparams(("parallel",), 32),
        name="ctx_attention",
    )(q, k, v)


def _outproj_body(*refs, n_parts):
    y_refs = refs[:n_parts]
    w_refs = refs[n_parts:2 * n_parts]
    h_ref, gt_ref, g_ref, o_ref = refs[2 * n_parts:]
    y = _dot(y_refs[0][0], w_refs[0][...])
    for yr, wr in zip(y_refs[1:], w_refs[1:]):
        y = y + _dot(yr[0], wr[...])
    o_ref[0] = h_ref[0] + gt_ref[0] * _rms(y, g_ref[0])


def _outproj(ys, ws, h, gt, g_post):
    b, l, d = h.shape
    tm = _tile(l, 1024)
    row = lambda bi, i: (bi, i, 0)
    vec = lambda bi, i: (bi, 0, 0)
    c2 = lambda bi, i: (0, 0)
    body = functools.partial(_outproj_body, n_parts=len(ys))
    return pl.pallas_call(
        body,
        grid=(b, l // tm),
        in_specs=([pl.BlockSpec((1, tm, y.shape[-1]), row) for y in ys]
                  + [pl.BlockSpec(w.shape, c2) for w in ws]
                  + [pl.BlockSpec((1, tm, d), row), _vec_spec(gt), _vec_spec(g_post)]),
        out_specs=pl.BlockSpec((1, tm, d), row),
        out_shape=jax.ShapeDtypeStruct((b, l, d), F32),
        compiler_params=_cparams(("parallel", "parallel"), 40),
        name="outproj",
    )(*ys, *ws, h, gt[0], g_post[0])


def _ffn_body(xp_ref, x_ref, xn_ref, sh_ref, sc_ref, gt_ref, gpre_ref, gpost_ref,
              wu_ref, cw_ref, cb_ref, wd_ref,
              o_ref, xs_ref, p3_ref, act_ref, acc_ref, *, tm, n_tiles, n_chunks, groups):
    i = pl.program_id(1)
    g = gpre_ref[0]
    sh = sh_ref[0]
    sc = sc_ref[0]
    nv = tm // SUBLANES
    nl = x_ref.shape[-1] // LANES
    xm = _modulate(x_ref[0], g, sh, sc)
    for j in range(nl):
        for s in range(SUBLANES):
            p3_ref[j, pl.ds(s, nv, stride=SUBLANES), :] = xm[s * nv:(s + 1) * nv, j * LANES:(j + 1) * LANES]
    xs_ref[0:tm] = jnp.concatenate([p3_ref[j] for j in range(nl)], axis=1).astype(BF16)
    rid = lax.broadcasted_iota(jnp.int32, (2 * SUBLANES, 1), 0)
    keep = jnp.where(rid == 0, (i > 0).astype(F32), jnp.where(rid == 1, (i < n_tiles - 1).astype(F32), 0.0))
    halo = jnp.concatenate([xp_ref[0, SUBLANES - 1:SUBLANES], xn_ref[0, 0:1],
                            jnp.zeros((2 * SUBLANES - 2, x_ref.shape[-1]), F32)], axis=0)
    xs_ref[tm:tm + 2 * SUBLANES] = (_modulate(halo, g, sh, sc) * keep).astype(BF16)

    srow = lax.broadcasted_iota(jnp.int32, (SUBLANES, FFN_CW), 0)

    def conv(hh, cw, cb):
        hm = hh[0:tm]
        first = jnp.where(srow == 0, hh[tm:tm + 1], pltpu.roll(hm[tm - SUBLANES:tm], 1, axis=0))
        last = jnp.where(srow == SUBLANES - 1, hh[tm + 1:tm + 2], pltpu.roll(hm[0:SUBLANES], SUBLANES - 1, axis=0))
        prev = jnp.concatenate([first, hm[0:tm - SUBLANES]], axis=0)
        nxt = jnp.concatenate([hm[SUBLANES:tm], last], axis=0)
        return prev * cw[0:1] + hm * cw[1:2] + nxt * cw[2:3] + cb

    ff = n_chunks * FFN_CW

    def up(c):
        xs = xs_ref[...]
        return (_dot(xs, wu_ref[:, c * FFN_CW:(c + 1) * FFN_CW]),
                _dot(xs, wu_ref[:, ff + c * FFN_CW:ff + (c + 1) * FFN_CW]))

    pending = up(0)
    k0 = 0
    for c in range(n_chunks):
        ha, hg = pending
        if c + 1 < n_chunks:
            pending = up(c + 1)
        cc = slice(c * FFN_CW, (c + 1) * FFN_CW)
        cg = slice(ff + c * FFN_CW, ff + (c + 1) * FFN_CW)
        a = conv(ha, cw_ref[:, cc], cb_ref[:, cc])
        gg = conv(hg, cw_ref[:, cg], cb_ref[:, cg])
        act_ref[:, cc] = (_silu(gg) * a).astype(BF16)
        if c + 1 in groups:
            k1 = (c + 1) * FFN_CW
            part = _dot(act_ref[:, k0:k1], wd_ref[k0:k1, :])
            if k0 == 0:
                acc_ref[...] = part
            else:
                acc_ref[...] += part
            k0 = k1
    f = gt_ref[0] * _rms(acc_ref[...], gpost_ref[0])
    for j in range(nl):
        p3_ref[j] = f[:, j * LANES:(j + 1) * LANES]
    for s in range(SUBLANES):
        fs = jnp.concatenate([p3_ref[j, pl.ds(s, nv, stride=SUBLANES), :] for j in range(nl)], axis=1)
        o_ref[0, s * nv:(s + 1) * nv, :] = x_ref[0, s * nv:(s + 1) * nv, :] + fs


def _ffn(h, sh, sc, gt, g_pre, g_post, wts):
    wu, cw, cb, wd = wts
    b, l, d = h.shape
    tm = _tile(l, 512)
    n_tiles = l // tm
    ff = wd.shape[0]
    n_chunks = ff // FFN_CW
    groups = (6, n_chunks)
    hb = tm // SUBLANES
    last_hb = l // SUBLANES - 1
    row = lambda bi, i: (bi, i, 0)
    vec = lambda bi, i: (bi, 0, 0)
    c2 = lambda bi, i: (0, 0)
    whole = lambda a: pl.BlockSpec(a.shape, lambda bi, i: (0,) * a.ndim, pipeline_mode=pl.Buffered(1))
    body = functools.partial(_ffn_body, tm=tm, n_tiles=n_tiles, n_chunks=n_chunks, groups=groups)
    return pl.pallas_call(
        body,
        grid=(b, n_tiles),
        in_specs=[pl.BlockSpec((1, SUBLANES, d), lambda bi, i: (bi, jnp.maximum(i * hb - 1, 0), 0)),
                  pl.BlockSpec((1, tm, d), row),
                  pl.BlockSpec((1, SUBLANES, d), lambda bi, i: (bi, jnp.minimum((i + 1) * hb, last_hb), 0)),
                  _vec_spec(sh), _vec_spec(sc), _vec_spec(gt), _vec_spec(g_pre), _vec_spec(g_post),
                  whole(wu), whole(cw), whole(cb), whole(wd)],
        out_specs=pl.BlockSpec((1, tm, d), row),
        out_shape=jax.ShapeDtypeStruct((b, l, d), F32),
        scratch_shapes=[pltpu.VMEM((tm + 2 * SUBLANES, d), BF16), pltpu.VMEM((d // LANES, tm, LANES), F32),
                        pltpu.VMEM((tm, ff), BF16), pltpu.VMEM((tm, d), F32)],
        compiler_params=_cparams(("parallel", "parallel"), 56),
        name="conv_ffn",
    )(h, h, h, sh[0], sc[0], gt[0], g_pre[0], g_post[0], wu, cw, cb, wd)


def _ffn_weights(w_up, conv_w, conv_b, w_down):
    return w_up.astype(BF16), conv_w, conv_b[None, :], w_down.astype(BF16)


def _inproj_c_body(h_ref, sh_ref, sc_ref, g_ref, wq_ref, wk_ref, wv_ref, gq_ref, gk_ref, bd_ref,
                   cos_ref, sin_ref, *out_refs, n_qtiles, sub, n_sub):
    bd = bd_ref[...]
    if n_qtiles:
        q_ref, k_ref, v_ref = out_refs
    else:
        k_ref, v_ref = out_refs
    projs = []
    for t in range(n_sub):
        xm = _modulate(h_ref[0, t * sub:(t + 1) * sub], g_ref[0], sh_ref[0], sc_ref[0]).astype(BF16)
        tq = _dot(xm, wq_ref[...]) if n_qtiles else None
        projs.append((tq, _dot(xm, wk_ref[...]), _dot(xm, wv_ref[...])))
    for t in range(n_sub):
        rows = slice(t * sub, (t + 1) * sub)
        tq, tk, tv = projs[t]
        _store_ones_padded(v_ref, rows, tv)
        tiles = [(tq[:, 2 * LANES * j:2 * LANES * j + LANES], tq[:, 2 * LANES * j + LANES:2 * LANES * (j + 1)])
                 for j in range(n_qtiles)] + [(tk[:, :LANES], tk[:, LANES:])]
        sq = jnp.concatenate([(t1 * t1 + t2 * t2).astype(BF16) for t1, t2 in tiles], axis=0)
        ss = _dot(sq, bd)
        cs = cos_ref[rows]
        sn = sin_ref[rows]
        for j, (t1, t2) in enumerate(tiles):
            is_q = j < n_qtiles
            o_ref = q_ref if is_q else k_ref
            gains = gq_ref[...] if is_q else gk_ref[...]
            c0 = 2 * LANES * j if is_q else 0
            r = lax.rsqrt(ss[j * sub:(j + 1) * sub] * (1.0 / HDIM) + EPS)
            a1 = t1 * r * gains[0:1]
            a2 = t2 * r * gains[1:2]
            o_ref[0, rows, c0:c0 + LANES] = (a1 * cs - a2 * sn).astype(BF16)
            o_ref[0, rows, c0 + LANES:c0 + 2 * LANES] = (a1 * sn + a2 * cs).astype(BF16)


def _inproj_c(h, sh, sc, g_pre, wq, wk, wv, gq, gk, bd, cos, sin, with_q):
    b, l, d = h.shape
    sub = _tile(l, 512)
    n_sub = 2 if l % (2 * sub) == 0 else 1
    tm = sub * n_sub
    row = lambda bi, i: (bi, i, 0)
    vec = lambda bi, i: (bi, 0, 0)
    c2 = lambda bi, i: (0, 0)
    nq, nkv = wq.shape[1], wk.shape[1]
    n_qtiles = nq // (2 * LANES) if with_q else 0
    body = functools.partial(_inproj_c_body, n_qtiles=n_qtiles, sub=sub, n_sub=n_sub)
    out_specs = [pl.BlockSpec((1, tm, nkv), row), pl.BlockSpec((1, tm, 2 * nkv), row)]
    out_shape = [jax.ShapeDtypeStruct((b, l, nkv), BF16), jax.ShapeDtypeStruct((b, l, 2 * nkv), BF16)]
    if with_q:
        out_specs = [pl.BlockSpec((1, tm, nq), row)] + out_specs
        out_shape = [jax.ShapeDtypeStruct((b, l, nq), BF16)] + out_shape
    return pl.pallas_call(
        body,
        grid=(b, l // tm),
        in_specs=[pl.BlockSpec((1, tm, d), row), _vec_spec(sh), _vec_spec(sc), _vec_spec(g_pre),
                  pl.BlockSpec(wq.shape, c2), pl.BlockSpec(wk.shape, c2), pl.BlockSpec(wv.shape, c2),
                  pl.BlockSpec(gq.shape, c2), pl.BlockSpec(gk.shape, c2), pl.BlockSpec(bd.shape, c2),
                  pl.BlockSpec((tm, LANES), lambda bi, i: (i, 0)),
                  pl.BlockSpec((tm, LANES), lambda bi, i: (i, 0))],
        out_specs=out_specs,
        out_shape=out_shape,
        compiler_params=_cparams(("parallel", "parallel"), 40),
        name="inproj_c",
    )(h, sh[0], sc[0], g_pre[0], wq, wk, wv, gq, gk, bd, cos, sin)


def _gqa_body(q_ref, k_ref, v_ref, kc_ref, vc_ref, o_ref, *, tq, n_sub):
    k = k_ref[0]
    kc = kc_ref[0]
    seg = (lax.broadcasted_iota(jnp.int32, (tq, 2 * LANES), 1) % LANES) // (HDIM // 2)
    lo = lax.broadcasted_iota(jnp.int32, (tq, LANES), 1) < HDIM
    def scores(t):
        q = q_ref[0, t * tq:(t + 1) * tq]
        zero = jnp.zeros_like(q)
        lhs = jnp.concatenate([jnp.where(seg == s, q, zero) for s in range(4)], axis=0)
        return _dot_t(lhs, k), _dot_t(lhs, kc)

    pending = scores(0)
    for t in range(n_sub):
        s_lat, s_ctx = pending
        if t + 1 < n_sub:
            pending = scores(t + 1)
        m = jnp.maximum(s_lat.max(axis=-1, keepdims=True), s_ctx.max(axis=-1, keepdims=True))
        p_lat = jnp.exp2(s_lat - m).astype(BF16)
        p_ctx = jnp.exp2(s_ctx - m).astype(BF16)
        outs = []
        for s in range(4):
            rs = slice(s * tq, (s + 1) * tq)
            cs = slice(s * LANES, (s + 1) * LANES)
            outs.append(_dot(p_lat[rs], v_ref[0, :, cs]) + _dot(p_ctx[rs], vc_ref[0, :, cs]))
        o_ref[0, t * tq:(t + 1) * tq] = jnp.concatenate(
            [_normalised_pair(outs[0], outs[1]), _normalised_pair(outs[2], outs[3])], axis=1).astype(BF16)


def _gqa_attention(q, k, v, kc, vc):
    b, s, nq = q.shape
    lc = kc.shape[1]
    nkv = k.shape[-1]
    n_sub = 2
    tq = 256 * n_sub
    body = functools.partial(_gqa_body, tq=tq // n_sub, n_sub=n_sub)
    kv_spec = lambda n, w: pl.BlockSpec((1, n, w), lambda bi, i, j: (bi, 0, 0))
    return pl.pallas_call(
        body,
        grid=(b, s // tq, nq // (2 * LANES)),
        in_specs=[pl.BlockSpec((1, tq, 2 * LANES), lambda bi, i, j: (bi, i, j)),
                  kv_spec(s, nkv), kv_spec(s, 2 * nkv), kv_spec(lc, nkv), kv_spec(lc, 2 * nkv)],
        out_specs=pl.BlockSpec((1, tq, 2 * LANES), lambda bi, i, j: (bi, i, j)),
        out_shape=jax.ShapeDtypeStruct((b, s, nq), BF16),
        compiler_params=_cparams(("parallel", "arbitrary", "arbitrary"), 48),
        name="gqa_attention",
    )(q, k, v, kc, vc)


def _na_bias_table(rpb):
    qc = np.arange(GRID_W)[:, None]
    kc = np.arange(GRID_W)[None, :]
    c0 = np.clip(qc - NA_COLS // 2, 0, GRID_W - NA_COLS)
    in_win = (kc >= c0) & (kc < c0 + NA_COLS)
    n_h, n_dr, n_dc = rpb.shape
    period = 2 * GRID_W
    lead = GRID_W - NA_COLS
    sig = jnp.pad(rpb, ((0, 0), (0, 0), (lead, period - n_dc - lead)))
    toep = jnp.tile(sig, (1, 1, GRID_W))[..., :GRID_W * (period - 1)].reshape(n_h, n_dr, GRID_W, period - 1)
    toep = toep[..., GRID_W - 1:]
    t = jnp.where(in_win, toep * LOG2E, NEG_INF)
    t = jnp.concatenate([t, jnp.full_like(t[:, :1], NEG_INF)], axis=1)
    return jnp.concatenate([t, t], axis=-1)


def _rope_tables(n_tokens):
    t = np.arange(n_tokens)
    quarter = HDIM // 4
    inv = (ROPE_THETA ** (-np.arange(quarter, dtype=np.float32) / quarter)).astype(np.float32)
    rows = (t // GRID_W).astype(np.float32)[:, None] * inv
    cols = (t % GRID_W).astype(np.float32)[:, None] * inv
    ang = np.tile(np.concatenate([rows, cols], axis=-1), (1, LANES // (HDIM // 2)))
    return jnp.asarray(np.cos(ang), F32), jnp.asarray(np.sin(ang), F32)


def kernel(x, c, ctx, c_ctx, w_ada, b_ada, norm_g, w_in_ab, a_w_s, a_b_s, a_v_g, b_rpb, w_out_ab,
           w_qkv_c, c_q_g, c_k_g, w_out_c, w_up, conv_w, conv_b, w_down):
    bsz, seq, d = x.shape
    lc = ctx.shape[1]
    depth = w_ada.shape[0]
    aw = a_v_g.shape[-1]
    n_heads_b = b_rpb.shape[1]
    bw = n_heads_b * HDIM
    n_kv = 4
    n_heads_c = w_out_c.shape[1] // HDIM
    assert seq % (NA_QROWS * GRID_W) == 0 and seq // GRID_W >= NA_KROWS
    assert aw // A_GROUPS == A_CHUNK and w_up.shape[-1] // 2 % FFN_CW == 0

    n_rows = 8 * ((bsz + 1 + 7) // 8)
    cv = jnp.concatenate([c, c_ctx[None], jnp.zeros((n_rows - bsz - 1, d), F32)], axis=0)
    mods = _ada(cv, w_ada, b_ada).reshape(depth * n_rows * 6, 1, d)
    gains = norm_g.reshape(depth * 4, 1, d)

    def mod_vec(layer, j, is_ctx):
        base = layer * n_rows * 6 + j
        return mods, ((lambda bi: base + bsz * 6) if is_ctx else (lambda bi: base + bi * 6))

    def gain_vec(layer, j):
        return gains, (lambda bi: layer * 4 + j)

    cos_l, sin_l = _rope_tables(seq)
    cos_c, sin_c = jnp.ones((lc, LANES), F32), jnp.zeros((lc, LANES), F32)

    h, hc = x, ctx
    for layer in range(depth):
        with_ctx = layer < depth - 1
        lat = [mod_vec(layer, j, False) for j in range(6)]
        cx = [mod_vec(layer, j, True) for j in range(6)]
        g_pre_m, g_post_m, g_pre_f, g_post_f = [gain_vec(layer, j) for j in range(4)]
        if layer % 2 == 0:
            e = layer // 2
            w_in = w_in_ab[e].astype(BF16)
            w_s = a_w_s[e].astype(BF16)
            b_s_b = jnp.broadcast_to(a_b_s[e][:, :, None], a_w_s[e].shape[:2] + (aw // A_GROUPS,))
            g_v = a_v_g[e][None, :]
            w_out = w_out_ab[e].astype(BF16)
            ya, q, k, v = _inproj_ab(h, lat[0], lat[1], g_pre_m, w_in, w_s, b_s_b, g_v, aw, bw)
            yca, qc, kc, vc = _inproj_ab(hc, cx[0], cx[1], g_pre_m, w_in, w_s, b_s_b, g_v, aw, bw)
            yb = _na_attention(q, k, v, kc, vc, _na_bias_table(b_rpb[e]))
            h = _outproj([ya, yb], [w_out[:aw], w_out[aw:]], h, lat[2], g_post_m)
            if with_ctx:
                ycb = _ctx_attention(qc, kc, vc)
                hc = _outproj([yca, ycb], [w_out[:aw], w_out[aw:]], hc, cx[2], g_post_m)
        else:
            o = layer // 2
            nq = n_heads_c * HDIM
            nkv = n_kv * HDIM
            per = n_heads_c // n_kv
            half = HDIM // 2
            wqkv = w_qkv_c[o].astype(BF16)
            wq = wqkv[:, :nq].reshape(d, n_kv, per, 2, half).transpose(0, 2, 3, 1, 4).reshape(d, nq)
            wk = wqkv[:, nq:nq + nkv].reshape(d, n_kv, 2, half).transpose(0, 2, 1, 3).reshape(d, nkv)
            wv = wqkv[:, nq + nkv:]
            reps = LANES // half
            gq = (jnp.stack([jnp.tile(c_q_g[o][:half], reps), jnp.tile(c_q_g[o][half:], reps)])
                  * (HDIM ** -0.5 * LOG2E))
            gk = jnp.stack([jnp.tile(c_k_g[o][:half], reps), jnp.tile(c_k_g[o][half:], reps)])
            seg = np.arange(LANES) // half
            bd = jnp.asarray(seg[:, None] == seg[None, :], BF16)
            w_out = w_out_c[o].astype(BF16).reshape(n_kv, per, HDIM, d).transpose(1, 0, 2, 3).reshape(nq, d)
            q, k, v = _inproj_c(h, lat[0], lat[1], g_pre_m, wq, wk, wv, gq, gk, bd, cos_l, sin_l, True)
            kc, vc = _inproj_c(hc, cx[0], cx[1], g_pre_m, wq, wk, wv, gq, gk, bd, cos_c, sin_c, False)
            y = _gqa_attention(q, k, v, kc, vc)
            h = _outproj([y], [w_out], h, lat[2], g_post_m)
            if with_ctx:
                raise NotImplementedError("context update after a grouped-query layer")
        wts = _ffn_weights(w_up[layer], conv_w[layer], conv_b[layer], w_down[layer])
        h = _ffn(h, lat[3], lat[4], lat[5], g_pre_f, g_post_f, wts)
        if with_ctx:
            hc = _ffn(hc, cx[3], cx[4], cx[5], g_pre_f, g_post_f, wts)
    return h
```

```python
import functools

import numpy as np
import jax
import jax.numpy as jnp
from jax import lax
from jax.experimental import pallas as pl
from jax.experimental.pallas import tpu as pltpu

F32 = jnp.float32
BF16 = jnp.bfloat16

EPS = 1e-6
NEG_INF = -1e30
GRID_W = 64
ROPE_THETA = 10000.0
LOG2E = 1.4426950408889634
A_CHUNK = 128
A_GROUPS = 4
NA_ROWS = 8
NA_COLS = 16
HDIM = 64
LANES = 128
NA_QROWS = 4
NA_KROWS = 12
SUBLANES = 8
FFN_CW = 256
V7X_VMEM_BYTES = 64 * 1024 * 1024


def _cparams(sem, vmem_mb):
    assert vmem_mb * 1024 * 1024 < V7X_VMEM_BYTES
    return pltpu.CompilerParams(dimension_semantics=sem, vmem_limit_bytes=vmem_mb * 1024 * 1024)


def _tile(n, pref):
    return pref if n % pref == 0 else n


def _vec_spec(vec):
    table, row = vec
    return pl.BlockSpec((1, 1, table.shape[-1]), lambda bi, *_: (row(bi), 0, 0))


def _rms(x, g):
    return x * lax.rsqrt(jnp.mean(x * x, axis=-1, keepdims=True) + EPS) * g


def _modulate(x, g, shift, scale):
    return _rms(x, g) * (1.0 + scale) + shift


def _gelu(x):
    return 0.5 * x * (1.0 + jnp.tanh(0.7978845608028654 * (x + 0.044715 * (x * x * x))))


def _silu(x):
    return x * jax.nn.sigmoid(x)


def _dot(a, b):
    return jnp.dot(a, b, preferred_element_type=F32)


def _dot_t(a, b):
    return lax.dot_general(a, b, (((1,), (1,)), ((), ())), preferred_element_type=F32)


def _ada_body(c_ref, w_ref, b_ref, o_ref):
    s = _silu(c_ref[...])
    o_ref[0] = jnp.dot(s, w_ref[0], preferred_element_type=F32,
                       precision=lax.Precision.HIGHEST) + b_ref[0]


def _ada(cv, w_ada, b_ada):
    depth, d, n = w_ada.shape
    rows = cv.shape[0]
    tn = _tile(n, 1536)
    return pl.pallas_call(
        _ada_body,
        grid=(depth, n // tn),
        in_specs=[pl.BlockSpec((rows, d), lambda l, j: (0, 0)),
                  pl.BlockSpec((1, d, tn), lambda l, j: (l, 0, j)),
                  pl.BlockSpec((1, 1, tn), lambda l, j: (l, 0, j))],
        out_specs=pl.BlockSpec((1, rows, tn), lambda l, j: (l, 0, j)),
        out_shape=jax.ShapeDtypeStruct((depth, rows, n), F32),
        compiler_params=_cparams(("parallel", "parallel"), 40),
        name="ada",
    )(cv, w_ada, b_ada.reshape(depth, 1, n))


def _store_ones_padded(v_ref, rows, vals):
    lo = lax.broadcasted_iota(jnp.int32, (vals.shape[0], LANES), 1) < HDIM
    for t in range(vals.shape[1] // LANES):
        vt = vals[:, t * LANES:(t + 1) * LANES]
        v_ref[0, rows, (2 * t) * LANES:(2 * t + 1) * LANES] = jnp.where(lo, vt, 1.0).astype(BF16)
        v_ref[0, rows, (2 * t + 1) * LANES:(2 * t + 2) * LANES] = jnp.where(lo, 1.0, vt).astype(BF16)


def _inproj_ab_body(h_ref, sh_ref, sc_ref, g_ref, w_ref, ws_ref, bs_ref, gv_ref,
                    ya_ref, q_ref, k_ref, v_ref, *, sub, n_sub, aw):
    bw = q_ref.shape[-1]
    gd = aw // A_GROUPS
    bounds = [0, aw, 2 * aw, 2 * aw + bw, 2 * aw + 2 * bw, 2 * aw + 3 * bw]
    projs = []
    for t in range(n_sub):
        xm = _modulate(h_ref[0, t * sub:(t + 1) * sub], g_ref[0], sh_ref[0], sc_ref[0]).astype(BF16)
        projs.append([_dot(xm, w_ref[:, c0:c1]) for c0, c1 in zip(bounds[:-1], bounds[1:])])
    for t in range(n_sub):
        rows = slice(t * sub, (t + 1) * sub)
        tu, tva, tq, tk, tv = projs[t]
        q_ref[0, rows] = (tq * (HDIM ** -0.5 * LOG2E)).astype(BF16)
        k_ref[0, rows] = tk.astype(BF16)
        _store_ones_padded(v_ref, rows, tv)
        u = _gelu(tu)
        va = _gelu(tva)
        mu = jnp.mean(va, axis=-1, keepdims=True)
        d = va - mu
        var = jnp.mean(d * d, axis=-1, keepdims=True)
        vn = (d * lax.rsqrt(var + EPS) * gv_ref[...]).astype(BF16)
        for n in range(sub // A_CHUNK):
            r0, r1 = n * A_CHUNK, (n + 1) * A_CHUNK
            for g in range(A_GROUPS):
                c0, c1 = g * gd, (g + 1) * gd
                s = _dot(ws_ref[g], vn[r0:r1, c0:c1]) + bs_ref[g]
                ya_ref[0, t * sub + r0:t * sub + r1, c0:c1] = (u[r0:r1, c0:c1] * s).astype(BF16)


def _inproj_ab(h, sh, sc, g_pre, w_in, w_s, b_s_b, g_v, aw, bw):
    b, l, d = h.shape
    sub = _tile(l, 512)
    n_sub = 2 if l % (2 * sub) == 0 else 1
    tm = sub * n_sub
    n_in = w_in.shape[1]
    row = lambda bi, i: (bi, i, 0)
    c2 = lambda bi, i: (0, 0)
    c3 = lambda bi, i: (0, 0, 0)
    body = functools.partial(_inproj_ab_body, sub=sub, n_sub=n_sub, aw=aw)
    widths = (aw, bw, bw, 2 * bw)
    return pl.pallas_call(
        body,
        grid=(b, l // tm),
        in_specs=[pl.BlockSpec((1, tm, d), row), _vec_spec(sh), _vec_spec(sc), _vec_spec(g_pre),
                  pl.BlockSpec((d, n_in), c2),
                  pl.BlockSpec(w_s.shape, c3), pl.BlockSpec(b_s_b.shape, c3),
                  pl.BlockSpec((1, aw), c2)],
        out_specs=[pl.BlockSpec((1, tm, w), row) for w in widths],
        out_shape=[jax.ShapeDtypeStruct((b, l, w), BF16) for w in widths],
        compiler_params=_cparams(("parallel", "parallel"), 48),
        name="inproj_ab",
    )(h, sh[0], sc[0], g_pre[0], w_in, w_s, b_s_b, g_v)


def _softmax_pv(scores, values):
    m = scores[0].max(axis=-1, keepdims=True)
    for s in scores[1:]:
        m = jnp.maximum(m, s.max(axis=-1, keepdims=True))
    o = _dot(jnp.exp2(scores[0] - m).astype(BF16), values[0])
    for s, v in zip(scores[1:], values[1:]):
        o = o + _dot(jnp.exp2(s - m).astype(BF16), v)
    return o


def _normalised_pair(o_even, o_odd):
    lo = lax.broadcasted_iota(jnp.int32, o_even.shape, 1) < HDIM
    return jnp.where(lo, o_even / pltpu.roll(o_even, HDIM, axis=1), o_odd / pltpu.roll(o_odd, HDIM, axis=1))


def _na_body(q_ref, k_ref, v_ref, kc_ref, vc_ref, t_ref, o_ref, *, rows, n_blocks):
    for u in range(n_blocks):
        _na_block(q_ref, k_ref, v_ref, kc_ref, vc_ref, t_ref, o_ref,
                  n_blocks * pl.program_id(1) + u, u * NA_QROWS * GRID_W, rows)


def _na_block(q_ref, k_ref, v_ref, kc_ref, vc_ref, t_ref, o_ref, rb, qoff, rows):
    tq = NA_QROWS * GRID_W
    nk = NA_KROWS * GRID_W
    start = jnp.clip(NA_QROWS * rb - NA_ROWS // 2, 0, rows - NA_KROWS)
    koff = pl.multiple_of(start * GRID_W, NA_QROWS * GRID_W)
    lo_q = lax.broadcasted_iota(jnp.int32, (tq, LANES), 1) < HDIM
    lo_b = lax.broadcasted_iota(jnp.int32, (GRID_W, LANES), 1) < GRID_W
    idx = []
    for ri in range(NA_QROWS):
        r = NA_QROWS * rb + ri
        r0 = jnp.clip(r - NA_ROWS // 2, 0, rows - NA_ROWS)
        row = []
        for j in range(NA_KROWS):
            kr = start + j
            valid = jnp.logical_and(kr >= r0, kr < r0 + NA_ROWS)
            row.append(jnp.where(valid, kr - r + NA_ROWS - 1, 2 * NA_ROWS - 1))
        idx.append(row)
    for p in range(q_ref.shape[-1] // LANES):
        cp = slice(p * LANES, (p + 1) * LANES)
        q = q_ref[0, qoff:qoff + tq, cp]
        zero = jnp.zeros_like(q)
        lhs = jnp.concatenate([jnp.where(lo_q, q, zero), jnp.where(lo_q, zero, q)], axis=0)
        bias = jnp.concatenate([
            jnp.concatenate([
                jnp.where(lo_b, t_ref[2 * p + a, idx[ri][2 * m]], t_ref[2 * p + a, idx[ri][2 * m + 1]])
                for m in range(NA_KROWS // 2)], axis=1)
            for a in range(2) for ri in range(NA_QROWS)], axis=0)
        s_loc = _dot_t(lhs, k_ref[0, pl.ds(koff, nk), cp]) + bias
        s_ctx = _dot_t(lhs, kc_ref[0, :, cp])
        m = jnp.maximum(s_loc.max(axis=-1, keepdims=True), s_ctx.max(axis=-1, keepdims=True))
        p_loc = jnp.exp2(s_loc - m).astype(BF16)
        p_ctx = jnp.exp2(s_ctx - m).astype(BF16)
        outs = []
        for a in range(2):
            ch = slice((2 * p + a) * LANES, (2 * p + a + 1) * LANES)
            rs = slice(a * tq, (a + 1) * tq)
            outs.append(_dot(p_loc[rs], v_ref[0, pl.ds(koff, nk), ch]) + _dot(p_ctx[rs], vc_ref[0, :, ch]))
        o_ref[0, qoff:qoff + tq, cp] = _normalised_pair(outs[0], outs[1]).astype(BF16)


def _na_attention(q, k, v, kc, vc, tdup):
    b, s, w = q.shape
    lc = kc.shape[1]
    rows = s // GRID_W
    n_blocks = 4
    assert rows % (n_blocks * NA_QROWS) == 0
    tq = n_blocks * NA_QROWS * GRID_W
    body = functools.partial(_na_body, rows=rows, n_blocks=n_blocks)
    whole = lambda n, wd: pl.BlockSpec((1, n, wd), lambda bi, r: (bi, 0, 0))
    return pl.pallas_call(
        body,
        grid=(b, rows // (n_blocks * NA_QROWS)),
        in_specs=[pl.BlockSpec((1, tq, w), lambda bi, r: (bi, r, 0)),
                  whole(s, w), whole(s, 2 * w), whole(lc, w), whole(lc, 2 * w),
                  pl.BlockSpec(tdup.shape, lambda bi, r: (0, 0, 0, 0))],
        out_specs=pl.BlockSpec((1, tq, w), lambda bi, r: (bi, r, 0)),
        out_shape=jax.ShapeDtypeStruct((b, s, w), BF16),
        compiler_params=_cparams(("parallel", "arbitrary"), 48),
        name="na_attention",
    )(q, k, v, kc, vc, tdup)


def _ctx_attn_body(q_ref, k_ref, v_ref, o_ref):
    lo = lax.broadcasted_iota(jnp.int32, (q_ref.shape[1], LANES), 1) < HDIM
    for p in range(q_ref.shape[-1] // LANES):
        cp = slice(p * LANES, (p + 1) * LANES)
        q = q_ref[0, :, cp]
        k = k_ref[0, :, cp]
        outs = []
        for a in range(2):
            ch = slice((2 * p + a) * LANES, (2 * p + a + 1) * LANES)
            qa = jnp.where(lo if a == 0 else jnp.logical_not(lo), q, jnp.zeros_like(q))
            outs.append(_softmax_pv([_dot_t(qa, k)], [v_ref[0, :, ch]]))
        o_ref[0, :, cp] = _normalised_pair(outs[0], outs[1]).astype(BF16)


def _ctx_attention(q, k, v):
    b, l, w = q.shape
    spec = pl.BlockSpec((1, l, w), lambda bi: (bi, 0, 0))
    return pl.pallas_call(
        _ctx_attn_body,
        grid=(b,),
        in_specs=[spec, spec, pl.BlockSpec((1, l, 2 * w), lambda bi: (bi, 0, 0))],
        out_specs=spec,
        out_shape=jax.ShapeDtypeStruct((b, l, w), BF16),
        compiler_params=_cparams(("parallel",), 32),
        name="ctx_attention",
    )(q, k, v)


def _outproj_body(*refs, n_parts):
    y_refs = refs[:n_parts]
    w_refs = refs[n_parts:2 * n_parts]
    h_ref, gt_ref, g_ref, o_ref = refs[2 * n_parts:]
    y = _dot(y_refs[0][0], w_refs[0][...])
    for yr, wr in zip(y_refs[1:], w_refs[1:]):
        y = y + _dot(yr[0], wr[...])
    o_ref[0] = h_ref[0] + gt_ref[0] * _rms(y, g_ref[0])


def _outproj(ys, ws, h, gt, g_post):
    b, l, d = h.shape
    tm = _tile(l, 1024)
    row = lambda bi, i: (bi, i, 0)
    c2 = lambda bi, i: (0, 0)
    body = functools.partial(_outproj_body, n_parts=len(ys))
    return pl.pallas_call(
        body,
        grid=(b, l // tm),
        in_specs=([pl.BlockSpec((1, tm, y.shape[-1]), row) for y in ys]
                  + [pl.BlockSpec(w.shape, c2) for w in ws]
                  + [pl.BlockSpec((1, tm, d), row), _vec_spec(gt), _vec_spec(g_post)]),
        out_specs=pl.BlockSpec((1, tm, d), row),
        out_shape=jax.ShapeDtypeStruct((b, l, d), F32),
        compiler_params=_cparams(("parallel", "parallel"), 40),
        name="outproj",
    )(*ys, *ws, h, gt[0], g_post[0])


def _ffn_body(xp_ref, x_ref, xn_ref, sh_ref, sc_ref, gt_ref, gpre_ref, gpost_ref,
              wu_ref, cw_ref, cb_ref, wd_ref,
              o_ref, xs_ref, p3_ref, act_ref, acc_ref, *, tm, n_tiles, n_chunks, groups):
    i = pl.program_id(1)
    g = gpre_ref[0]
    sh = sh_ref[0]
    sc = sc_ref[0]
    nv = tm // SUBLANES
    nl = x_ref.shape[-1] // LANES
    xm = _modulate(x_ref[0], g, sh, sc)
    for j in range(nl):
        for s in range(SUBLANES):
            p3_ref[j, pl.ds(s, nv, stride=SUBLANES), :] = xm[s * nv:(s + 1) * nv, j * LANES:(j + 1) * LANES]
    xs_ref[0:tm] = jnp.concatenate([p3_ref[j] for j in range(nl)], axis=1).astype(BF16)
    rid = lax.broadcasted_iota(jnp.int32, (2 * SUBLANES, 1), 0)
    keep = jnp.where(rid == 0, (i > 0).astype(F32), jnp.where(rid == 1, (i < n_tiles - 1).astype(F32), 0.0))
    halo = jnp.concatenate([xp_ref[0, SUBLANES - 1:SUBLANES], xn_ref[0, 0:1],
                            jnp.zeros((2 * SUBLANES - 2, x_ref.shape[-1]), F32)], axis=0)
    xs_ref[tm:tm + 2 * SUBLANES] = (_modulate(halo, g, sh, sc) * keep).astype(BF16)

    srow = lax.broadcasted_iota(jnp.int32, (SUBLANES, FFN_CW), 0)

    def conv(hh, cw, cb):
        hm = hh[0:tm]
        first = jnp.where(srow == 0, hh[tm:tm + 1], pltpu.roll(hm[tm - SUBLANES:tm], 1, axis=0))
        last = jnp.where(srow == SUBLANES - 1, hh[tm + 1:tm + 2], pltpu.roll(hm[0:SUBLANES], SUBLANES - 1, axis=0))
        prev = jnp.concatenate([first, hm[0:tm - SUBLANES]], axis=0)
        nxt = jnp.concatenate([hm[SUBLANES:tm], last], axis=0)
        return prev * cw[0:1] + hm * cw[1:2] + nxt * cw[2:3] + cb

    ff = n_chunks * FFN_CW

    def up(c):
        xs = xs_ref[...]
        return (_dot(xs, wu_ref[:, c * FFN_CW:(c + 1) * FFN_CW]),
                _dot(xs, wu_ref[:, ff + c * FFN_CW:ff + (c + 1) * FFN_CW]))

    pending = up(0)
    k0 = 0
    for c in range(n_chunks):
        ha, hg = pending
        if c + 1 < n_chunks:
            pending = up(c + 1)
        cc = slice(c * FFN_CW, (c + 1) * FFN_CW)
        cg = slice(ff + c * FFN_CW, ff + (c + 1) * FFN_CW)
        a = conv(ha, cw_ref[:, cc], cb_ref[:, cc])
        gg = conv(hg, cw_ref[:, cg], cb_ref[:, cg])
        act_ref[:, cc] = (_silu(gg) * a).astype(BF16)
        if c + 1 in groups:
            k1 = (c + 1) * FFN_CW
            part = _dot(act_ref[:, k0:k1], wd_ref[k0:k1, :])
            if k0 == 0:
                acc_ref[...] = part
            else:
                acc_ref[...] += part
            k0 = k1
    f = gt_ref[0] * _rms(acc_ref[...], gpost_ref[0])
    for j in range(nl):
        p3_ref[j] = f[:, j * LANES:(j + 1) * LANES]
    for s in range(SUBLANES):
        fs = jnp.concatenate([p3_ref[j, pl.ds(s, nv, stride=SUBLANES), :] for j in range(nl)], axis=1)
        o_ref[0, s * nv:(s + 1) * nv, :] = x_ref[0, s * nv:(s + 1) * nv, :] + fs


def _ffn(h, sh, sc, gt, g_pre, g_post, wts):
    wu, cw, cb, wd = wts
    b, l, d = h.shape
    tm = _tile(l, 512)
    n_tiles = l // tm
    ff = wd.shape[0]
    n_chunks = ff // FFN_CW
    groups = (6, n_chunks)
    hb = tm // SUBLANES
    last_hb = l // SUBLANES - 1
    row = lambda bi, i: (bi, i, 0)
    c2 = lambda bi, i: (0, 0)
    whole = lambda a: pl.BlockSpec(a.shape, lambda bi, i: (0,) * a.ndim, pipeline_mode=pl.Buffered(1))
    body = functools.partial(_ffn_body, tm=tm, n_tiles=n_tiles, n_chunks=n_chunks, groups=groups)
    return pl.pallas_call(
        body,
        grid=(b, n_tiles),
        in_specs=[pl.BlockSpec((1, SUBLANES, d), lambda bi, i: (bi, jnp.maximum(i * hb - 1, 0), 0)),
                  pl.BlockSpec((1, tm, d), row),
                  pl.BlockSpec((1, SUBLANES, d), lambda bi, i: (bi, jnp.minimum((i + 1) * hb, last_hb), 0)),
                  _vec_spec(sh), _vec_spec(sc), _vec_spec(gt), _vec_spec(g_pre), _vec_spec(g_post),
                  whole(wu), whole(cw), whole(cb), whole(wd)],
        out_specs=pl.BlockSpec((1, tm, d), row),
        out_shape=jax.ShapeDtypeStruct((b, l, d), F32),
        scratch_shapes=[pltpu.VMEM((tm + 2 * SUBLANES, d), BF16), pltpu.VMEM((d // LANES, tm, LANES), F32),
                        pltpu.VMEM((tm, ff), BF16), pltpu.VMEM((tm, d), F32)],
        compiler_params=_cparams(("parallel", "parallel"), 56),
        name="conv_ffn",
    )(h, h, h, sh[0], sc[0], gt[0], g_pre[0], g_post[0], wu, cw, cb, wd)


def _ffn_weights(w_up, conv_w, conv_b, w_down):
    return w_up.astype(BF16), conv_w, conv_b[None, :], w_down.astype(BF16)


def _inproj_c_body(h_ref, sh_ref, sc_ref, g_ref, wq_ref, wk_ref, wv_ref, gq_ref, gk_ref, bd_ref,
                   cos_ref, sin_ref, *out_refs, n_qtiles, sub, n_sub):
    bd = bd_ref[...]
    if n_qtiles:
        q_ref, k_ref, v_ref = out_refs
    else:
        k_ref, v_ref = out_refs
    projs = []
    for t in range(n_sub):
        xm = _modulate(h_ref[0, t * sub:(t + 1) * sub], g_ref[0], sh_ref[0], sc_ref[0]).astype(BF16)
        tq = _dot(xm, wq_ref[...]) if n_qtiles else None
        projs.append((tq, _dot(xm, wk_ref[...]), _dot(xm, wv_ref[...])))
    for t in range(n_sub):
        rows = slice(t * sub, (t + 1) * sub)
        tq, tk, tv = projs[t]
        _store_ones_padded(v_ref, rows, tv)
        tiles = [(tq[:, 2 * LANES * j:2 * LANES * j + LANES], tq[:, 2 * LANES * j + LANES:2 * LANES * (j + 1)])
                 for j in range(n_qtiles)] + [(tk[:, :LANES], tk[:, LANES:])]
        sq = jnp.concatenate([(t1 * t1 + t2 * t2).astype(BF16) for t1, t2 in tiles], axis=0)
        ss = _dot(sq, bd)
        cs = cos_ref[rows]
        sn = sin_ref[rows]
        for j, (t1, t2) in enumerate(tiles):
            is_q = j < n_qtiles
            o_ref = q_ref if is_q else k_ref
            gains = gq_ref[...] if is_q else gk_ref[...]
            c0 = 2 * LANES * j if is_q else 0
            r = lax.rsqrt(ss[j * sub:(j + 1) * sub] * (1.0 / HDIM) + EPS)
            a1 = t1 * r * gains[0:1]
            a2 = t2 * r * gains[1:2]
            o_ref[0, rows, c0:c0 + LANES] = (a1 * cs - a2 * sn).astype(BF16)
            o_ref[0, rows, c0 + LANES:c0 + 2 * LANES] = (a1 * sn + a2 * cs).astype(BF16)


def _inproj_c(h, sh, sc, g_pre, wq, wk, wv, gq, gk, bd, cos, sin, with_q):
    b, l, d = h.shape
    sub = _tile(l, 512)
    n_sub = 2 if l % (2 * sub) == 0 else 1
    tm = sub * n_sub
    row = lambda bi, i: (bi, i, 0)
    c2 = lambda bi, i: (0, 0)
    nq, nkv = wq.shape[1], wk.shape[1]
    n_qtiles = nq // (2 * LANES) if with_q else 0
    body = functools.partial(_inproj_c_body, n_qtiles=n_qtiles, sub=sub, n_sub=n_sub)
    out_specs = [pl.BlockSpec((1, tm, nkv), row), pl.BlockSpec((1, tm, 2 * nkv), row)]
    out_shape = [jax.ShapeDtypeStruct((b, l, nkv), BF16), jax.ShapeDtypeStruct((b, l, 2 * nkv), BF16)]
    if with_q:
        out_specs = [pl.BlockSpec((1, tm, nq), row)] + out_specs
        out_shape = [jax.ShapeDtypeStruct((b, l, nq), BF16)] + out_shape
    return pl.pallas_call(
        body,
        grid=(b, l // tm),
        in_specs=[pl.BlockSpec((1, tm, d), row), _vec_spec(sh), _vec_spec(sc), _vec_spec(g_pre),
                  pl.BlockSpec(wq.shape, c2), pl.BlockSpec(wk.shape, c2), pl.BlockSpec(wv.shape, c2),
                  pl.BlockSpec(gq.shape, c2), pl.BlockSpec(gk.shape, c2), pl.BlockSpec(bd.shape, c2),
                  pl.BlockSpec((tm, LANES), lambda bi, i: (i, 0)),
                  pl.BlockSpec((tm, LANES), lambda bi, i: (i, 0))],
        out_specs=out_specs,
        out_shape=out_shape,
        compiler_params=_cparams(("parallel", "parallel"), 40),
        name="inproj_c",
    )(h, sh[0], sc[0], g_pre[0], wq, wk, wv, gq, gk, bd, cos, sin)


def _gqa_body(q_ref, k_ref, v_ref, kc_ref, vc_ref, o_ref, *, tq, n_sub):
    k = k_ref[0]
    kc = kc_ref[0]
    seg = (lax.broadcasted_iota(jnp.int32, (tq, 2 * LANES), 1) % LANES) // (HDIM // 2)
    lo = lax.broadcasted_iota(jnp.int32, (tq, LANES), 1) < HDIM
    def scores(t):
        q = q_ref[0, t * tq:(t + 1) * tq]
        zero = jnp.zeros_like(q)
        lhs = jnp.concatenate([jnp.where(seg == s, q, zero) for s in range(4)], axis=0)
        return _dot_t(lhs, k), _dot_t(lhs, kc)

    pending = scores(0)
    for t in range(n_sub):
        s_lat, s_ctx = pending
        if t + 1 < n_sub:
            pending = scores(t + 1)
        m = jnp.maximum(s_lat.max(axis=-1, keepdims=True), s_ctx.max(axis=-1, keepdims=True))
        p_lat = jnp.exp2(s_lat - m).astype(BF16)
        p_ctx = jnp.exp2(s_ctx - m).astype(BF16)
        outs = []
        for s in range(4):
            rs = slice(s * tq, (s + 1) * tq)
            cs = slice(s * LANES, (s + 1) * LANES)
            outs.append(_dot(p_lat[rs], v_ref[0, :, cs]) + _dot(p_ctx[rs], vc_ref[0, :, cs]))
        o_ref[0, t * tq:(t + 1) * tq] = jnp.concatenate(
            [_normalised_pair(outs[0], outs[1]), _normalised_pair(outs[2], outs[3])], axis=1).astype(BF16)


def _gqa_attention(q, k, v, kc, vc):
    b, s, nq = q.shape
    lc = kc.shape[1]
    nkv = k.shape[-1]
    n_sub = 4
    tq = 256 * n_sub
    body = functools.partial(_gqa_body, tq=tq // n_sub, n_sub=n_sub)
    kv_spec = lambda n, w: pl.BlockSpec((1, n, w), lambda bi, i, j: (bi, 0, 0))
    return pl.pallas_call(
        body,
        grid=(b, s // tq, nq // (2 * LANES)),
        in_specs=[pl.BlockSpec((1, tq, 2 * LANES), lambda bi, i, j: (bi, i, j)),
                  kv_spec(s, nkv), kv_spec(s, 2 * nkv), kv_spec(lc, nkv), kv_spec(lc, 2 * nkv)],
        out_specs=pl.BlockSpec((1, tq, 2 * LANES), lambda bi, i, j: (bi, i, j)),
        out_shape=jax.ShapeDtypeStruct((b, s, nq), BF16),
        compiler_params=_cparams(("parallel", "arbitrary", "arbitrary"), 56),
        name="gqa_attention",
    )(q, k, v, kc, vc)


def _na_bias_table(rpb):
    qc = np.arange(GRID_W)[:, None]
    kc = np.arange(GRID_W)[None, :]
    c0 = np.clip(qc - NA_COLS // 2, 0, GRID_W - NA_COLS)
    in_win = (kc >= c0) & (kc < c0 + NA_COLS)
    n_h, n_dr, n_dc = rpb.shape
    period = 2 * GRID_W
    lead = GRID_W - NA_COLS
    sig = jnp.pad(rpb, ((0, 0), (0, 0), (lead, period - n_dc - lead)))
    toep = jnp.tile(sig, (1, 1, GRID_W))[..., :GRID_W * (period - 1)].reshape(n_h, n_dr, GRID_W, period - 1)
    toep = toep[..., GRID_W - 1:]
    t = jnp.where(in_win, toep * LOG2E, NEG_INF)
    t = jnp.concatenate([t, jnp.full_like(t[:, :1], NEG_INF)], axis=1)
    return jnp.concatenate([t, t], axis=-1)


def _rope_tables(n_tokens):
    t = np.arange(n_tokens)
    quarter = HDIM // 4
    inv = (ROPE_THETA ** (-np.arange(quarter, dtype=np.float32) / quarter)).astype(np.float32)
    rows = (t // GRID_W).astype(np.float32)[:, None] * inv
    cols = (t % GRID_W).astype(np.float32)[:, None] * inv
    ang = np.tile(np.concatenate([rows, cols], axis=-1), (1, LANES // (HDIM // 2)))
    return jnp.asarray(np.cos(ang), F32), jnp.asarray(np.sin(ang), F32)


def kernel(x, c, ctx, c_ctx, w_ada, b_ada, norm_g, w_in_ab, a_w_s, a_b_s, a_v_g, b_rpb, w_out_ab,
           w_qkv_c, c_q_g, c_k_g, w_out_c, w_up, conv_w, conv_b, w_down):
    bsz, seq, d = x.shape
    lc = ctx.shape[1]
    depth = w_ada.shape[0]
    aw = a_v_g.shape[-1]
    n_heads_b = b_rpb.shape[1]
    bw = n_heads_b * HDIM
    n_kv = 4
    n_heads_c = w_out_c.shape[1] // HDIM
    assert seq % (NA_QROWS * GRID_W) == 0 and seq // GRID_W >= NA_KROWS
    assert aw // A_GROUPS == A_CHUNK and w_up.shape[-1] // 2 % FFN_CW == 0

    n_rows = 8 * ((bsz + 1 + 7) // 8)
    cv = jnp.concatenate([c, c_ctx[None], jnp.zeros((n_rows - bsz - 1, d), F32)], axis=0)
    mods = _ada(cv, w_ada, b_ada).reshape(depth * n_rows * 6, 1, d)
    gains = norm_g.reshape(depth * 4, 1, d)

    def mod_vec(layer, j, is_ctx):
        base = layer * n_rows * 6 + j
        return mods, ((lambda bi: base + bsz * 6) if is_ctx else (lambda bi: base + bi * 6))

    def gain_vec(layer, j):
        return gains, (lambda bi: layer * 4 + j)

    cos_l, sin_l = _rope_tables(seq)
    cos_c, sin_c = jnp.ones((lc, LANES), F32), jnp.zeros((lc, LANES), F32)

    h, hc = x, ctx
    for layer in range(depth):
        with_ctx = layer < depth - 1
        lat = [mod_vec(layer, j, False) for j in range(6)]
        cx = [mod_vec(layer, j, True) for j in range(6)]
        g_pre_m, g_post_m, g_pre_f, g_post_f = [gain_vec(layer, j) for j in range(4)]
        if layer % 2 == 0:
            e = layer // 2
            w_in = w_in_ab[e].astype(BF16)
            w_s = a_w_s[e].astype(BF16)
            b_s_b = jnp.broadcast_to(a_b_s[e][:, :, None], a_w_s[e].shape[:2] + (aw // A_GROUPS,))
            g_v = a_v_g[e][None, :]
            w_out = w_out_ab[e].astype(BF16)
            ya, q, k, v = _inproj_ab(h, lat[0], lat[1], g_pre_m, w_in, w_s, b_s_b, g_v, aw, bw)
            yca, qc, kc, vc = _inproj_ab(hc, cx[0], cx[1], g_pre_m, w_in, w_s, b_s_b, g_v, aw, bw)
            yb = _na_attention(q, k, v, kc, vc, _na_bias_table(b_rpb[e]))
            h = _outproj([ya, yb], [w_out[:aw], w_out[aw:]], h, lat[2], g_post_m)
            if with_ctx:
                ycb = _ctx_attention(qc, kc, vc)
                hc = _outproj([yca, ycb], [w_out[:aw], w_out[aw:]], hc, cx[2], g_post_m)
        else:
            o = layer // 2
            nq = n_heads_c * HDIM
            nkv = n_kv * HDIM
            per = n_heads_c // n_kv
            half = HDIM // 2
            wqkv = w_qkv_c[o].astype(BF16)
            wq = wqkv[:, :nq].reshape(d, n_kv, per, 2, half).transpose(0, 2, 3, 1, 4).reshape(d, nq)
            wk = wqkv[:, nq:nq + nkv].reshape(d, n_kv, 2, half).transpose(0, 2, 1, 3).reshape(d, nkv)
            wv = wqkv[:, nq + nkv:]
            reps = LANES // half
            gq = (jnp.stack([jnp.tile(c_q_g[o][:half], reps), jnp.tile(c_q_g[o][half:], reps)])
                  * (HDIM ** -0.5 * LOG2E))
            gk = jnp.stack([jnp.tile(c_k_g[o][:half], reps), jnp.tile(c_k_g[o][half:], reps)])
            seg = np.arange(LANES) // half
            bd = jnp.asarray(seg[:, None] == seg[None, :], BF16)
            w_out = w_out_c[o].astype(BF16).reshape(n_kv, per, HDIM, d).transpose(1, 0, 2, 3).reshape(nq, d)
            q, k, v = _inproj_c(h, lat[0], lat[1], g_pre_m, wq, wk, wv, gq, gk, bd, cos_l, sin_l, True)
            kc, vc = _inproj_c(hc, cx[0], cx[1], g_pre_m, wq, wk, wv, gq, gk, bd, cos_c, sin_c, False)
            y = _gqa_attention(q, k, v, kc, vc)
            h = _outproj([y], [w_out], h, lat[2], g_post_m)
            if with_ctx:
                raise NotImplementedError("context update after a grouped-query layer")
        wts = _ffn_weights(w_up[layer], conv_w[layer], conv_b[layer], w_down[layer])
        h = _ffn(h, lat[3], lat[4], lat[5], g_pre_f, g_post_f, wts)
        if with_ctx:
            hc = _ffn(hc, cx[3], cx[4], cx[5], g_pre_f, g_post_f, wts)
    return h
```

```python
import functools

import numpy as np
import jax
import jax.numpy as jnp
from jax import lax
from jax.experimental import pallas as pl
from jax.experimental.pallas import tpu as pltpu

F32 = jnp.float32
BF16 = jnp.bfloat16

EPS = 1e-6
NEG_INF = -1e30
GRID_W = 64
ROPE_THETA = 10000.0
LOG2E = 1.4426950408889634
A_CHUNK = 128
A_GROUPS = 4
NA_ROWS = 8
NA_COLS = 16
HDIM = 64
LANES = 128
NA_QROWS = 4
NA_KROWS = 12
SUBLANES = 8
FFN_CW = 256
V7X_VMEM_BYTES = 64 * 1024 * 1024


def _cparams(sem, vmem_mb):
    assert vmem_mb * 1024 * 1024 < V7X_VMEM_BYTES
    return pltpu.CompilerParams(dimension_semantics=sem, vmem_limit_bytes=vmem_mb * 1024 * 1024)


def _tile(n, pref):
    return pref if n % pref == 0 else n


def _vec_spec(vec):
    table, row = vec
    return pl.BlockSpec((1, 1, table.shape[-1]), lambda bi, *_: (row(bi), 0, 0))


def _rms(x, g):
    return x * lax.rsqrt(jnp.mean(x * x, axis=-1, keepdims=True) + EPS) * g


def _modulate(x, g, shift, scale):
    return _rms(x, g) * (1.0 + scale) + shift


def _gelu(x):
    return 0.5 * x * (1.0 + jnp.tanh(0.7978845608028654 * (x + 0.044715 * (x * x * x))))


def _silu(x):
    return x * jax.nn.sigmoid(x)


def _dot(a, b):
    return jnp.dot(a, b, preferred_element_type=F32)


def _dot_t(a, b):
    return lax.dot_general(a, b, (((1,), (1,)), ((), ())), preferred_element_type=F32)


def _ada_body(c_ref, w_ref, b_ref, o_ref):
    s = _silu(c_ref[...])
    o_ref[0] = jnp.dot(s, w_ref[0], preferred_element_type=F32,
                       precision=lax.Precision.HIGHEST) + b_ref[0]


def _ada(cv, w_ada, b_ada):
    depth, d, n = w_ada.shape
    rows = cv.shape[0]
    tn = _tile(n, 1536)
    return pl.pallas_call(
        _ada_body,
        grid=(depth, n // tn),
        in_specs=[pl.BlockSpec((rows, d), lambda l, j: (0, 0)),
                  pl.BlockSpec((1, d, tn), lambda l, j: (l, 0, j)),
                  pl.BlockSpec((1, 1, tn), lambda l, j: (l, 0, j))],
        out_specs=pl.BlockSpec((1, rows, tn), lambda l, j: (l, 0, j)),
        out_shape=jax.ShapeDtypeStruct((depth, rows, n), F32),
        compiler_params=_cparams(("parallel", "parallel"), 40),
        name="ada",
    )(cv, w_ada, b_ada.reshape(depth, 1, n))


def _store_ones_padded(v_ref, rows, vals):
    lo = lax.broadcasted_iota(jnp.int32, (vals.shape[0], LANES), 1) < HDIM
    for t in range(vals.shape[1] // LANES):
        vt = vals[:, t * LANES:(t + 1) * LANES]
        v_ref[0, rows, (2 * t) * LANES:(2 * t + 1) * LANES] = jnp.where(lo, vt, 1.0).astype(BF16)
        v_ref[0, rows, (2 * t + 1) * LANES:(2 * t + 2) * LANES] = jnp.where(lo, 1.0, vt).astype(BF16)


def _store_ones_padded_t(vt_ref, cols, vals):
    lo = lax.broadcasted_iota(jnp.int32, (vals.shape[0], LANES), 1) < HDIM
    for t in range(vals.shape[1] // LANES):
        vt = vals[:, t * LANES:(t + 1) * LANES]
        vt_ref[0, (2 * t) * LANES:(2 * t + 1) * LANES, cols] = jnp.where(lo, vt, 1.0).T.astype(BF16)
        vt_ref[0, (2 * t + 1) * LANES:(2 * t + 2) * LANES, cols] = jnp.where(lo, 1.0, vt).T.astype(BF16)


def _inproj_ab_body(h_ref, sh_ref, sc_ref, g_ref, w_ref, ws_ref, bs_ref, gv_ref,
                    ya_ref, q_ref, k_ref, v_ref, *, sub, n_sub, aw):
    bw = q_ref.shape[-1]
    gd = aw // A_GROUPS
    bounds = [0, aw, 2 * aw, 2 * aw + bw, 2 * aw + 2 * bw, 2 * aw + 3 * bw]
    projs = []
    for t in range(n_sub):
        xm = _modulate(h_ref[0, t * sub:(t + 1) * sub], g_ref[0], sh_ref[0], sc_ref[0]).astype(BF16)
        projs.append([_dot(xm, w_ref[:, c0:c1]) for c0, c1 in zip(bounds[:-1], bounds[1:])])
    for t in range(n_sub):
        rows = slice(t * sub, (t + 1) * sub)
        tu, tva, tq, tk, tv = projs[t]
        q_ref[0, rows] = (tq * (HDIM ** -0.5 * LOG2E)).astype(BF16)
        k_ref[0, rows] = tk.astype(BF16)
        _store_ones_padded(v_ref, rows, tv)
        u = _gelu(tu)
        va = _gelu(tva)
        mu = jnp.mean(va, axis=-1, keepdims=True)
        d = va - mu
        var = jnp.mean(d * d, axis=-1, keepdims=True)
        vn = (d * lax.rsqrt(var + EPS) * gv_ref[...]).astype(BF16)
        for n in range(sub // A_CHUNK):
            r0, r1 = n * A_CHUNK, (n + 1) * A_CHUNK
            for g in range(A_GROUPS):
                c0, c1 = g * gd, (g + 1) * gd
                s = _dot(ws_ref[g], vn[r0:r1, c0:c1]) + bs_ref[g]
                ya_ref[0, t * sub + r0:t * sub + r1, c0:c1] = (u[r0:r1, c0:c1] * s).astype(BF16)


def _inproj_ab(h, sh, sc, g_pre, w_in, w_s, b_s_b, g_v, aw, bw):
    b, l, d = h.shape
    sub = _tile(l, 512)
    n_sub = 2 if l % (2 * sub) == 0 else 1
    tm = sub * n_sub
    n_in = w_in.shape[1]
    row = lambda bi, i: (bi, i, 0)
    c2 = lambda bi, i: (0, 0)
    c3 = lambda bi, i: (0, 0, 0)
    body = functools.partial(_inproj_ab_body, sub=sub, n_sub=n_sub, aw=aw)
    widths = (aw, bw, bw, 2 * bw)
    return pl.pallas_call(
        body,
        grid=(b, l // tm),
        in_specs=[pl.BlockSpec((1, tm, d), row), _vec_spec(sh), _vec_spec(sc), _vec_spec(g_pre),
                  pl.BlockSpec((d, n_in), c2),
                  pl.BlockSpec(w_s.shape, c3), pl.BlockSpec(b_s_b.shape, c3),
                  pl.BlockSpec((1, aw), c2)],
        out_specs=[pl.BlockSpec((1, tm, w), row) for w in widths],
        out_shape=[jax.ShapeDtypeStruct((b, l, w), BF16) for w in widths],
        compiler_params=_cparams(("parallel", "parallel"), 48),
        name="inproj_ab",
    )(h, sh[0], sc[0], g_pre[0], w_in, w_s, b_s_b, g_v)


def _softmax_pv(scores, values):
    m = scores[0].max(axis=-1, keepdims=True)
    for s in scores[1:]:
        m = jnp.maximum(m, s.max(axis=-1, keepdims=True))
    o = _dot(jnp.exp2(scores[0] - m).astype(BF16), values[0])
    for s, v in zip(scores[1:], values[1:]):
        o = o + _dot(jnp.exp2(s - m).astype(BF16), v)
    return o


def _normalised_pair(o_even, o_odd):
    lo = lax.broadcasted_iota(jnp.int32, o_even.shape, 1) < HDIM
    return jnp.where(lo, o_even / pltpu.roll(o_even, HDIM, axis=1), o_odd / pltpu.roll(o_odd, HDIM, axis=1))


def _na_body(q_ref, k_ref, v_ref, kc_ref, vc_ref, t_ref, o_ref, *, rows, n_blocks):
    for u in range(n_blocks):
        _na_block(q_ref, k_ref, v_ref, kc_ref, vc_ref, t_ref, o_ref,
                  n_blocks * pl.program_id(1) + u, u * NA_QROWS * GRID_W, rows)


def _na_block(q_ref, k_ref, v_ref, kc_ref, vc_ref, t_ref, o_ref, rb, qoff, rows):
    tq = NA_QROWS * GRID_W
    nk = NA_KROWS * GRID_W
    start = jnp.clip(NA_QROWS * rb - NA_ROWS // 2, 0, rows - NA_KROWS)
    koff = pl.multiple_of(start * GRID_W, NA_QROWS * GRID_W)
    lo_q = lax.broadcasted_iota(jnp.int32, (tq, LANES), 1) < HDIM
    lo_b = lax.broadcasted_iota(jnp.int32, (GRID_W, LANES), 1) < GRID_W
    idx = []
    for ri in range(NA_QROWS):
        r = NA_QROWS * rb + ri
        r0 = jnp.clip(r - NA_ROWS // 2, 0, rows - NA_ROWS)
        row = []
        for j in range(NA_KROWS):
            kr = start + j
            valid = jnp.logical_and(kr >= r0, kr < r0 + NA_ROWS)
            row.append(jnp.where(valid, kr - r + NA_ROWS - 1, 2 * NA_ROWS - 1))
        idx.append(row)
    for p in range(q_ref.shape[-1] // LANES):
        cp = slice(p * LANES, (p + 1) * LANES)
        q = q_ref[0, qoff:qoff + tq, cp]
        zero = jnp.zeros_like(q)
        lhs = jnp.concatenate([jnp.where(lo_q, q, zero), jnp.where(lo_q, zero, q)], axis=0)
        bias = jnp.concatenate([
            jnp.concatenate([
                jnp.where(lo_b, t_ref[2 * p + a, idx[ri][2 * m]], t_ref[2 * p + a, idx[ri][2 * m + 1]])
                for m in range(NA_KROWS // 2)], axis=1)
            for a in range(2) for ri in range(NA_QROWS)], axis=0)
        s_loc = _dot_t(lhs, k_ref[0, pl.ds(koff, nk), cp]) + bias
        s_ctx = _dot_t(lhs, kc_ref[0, :, cp])
        m = jnp.maximum(s_loc.max(axis=-1, keepdims=True), s_ctx.max(axis=-1, keepdims=True))
        p_loc = jnp.exp2(s_loc - m).astype(BF16)
        p_ctx = jnp.exp2(s_ctx - m).astype(BF16)
        outs = []
        for a in range(2):
            ch = slice((2 * p + a) * LANES, (2 * p + a + 1) * LANES)
            rs = slice(a * tq, (a + 1) * tq)
            outs.append(_dot(p_loc[rs], v_ref[0, pl.ds(koff, nk), ch]) + _dot(p_ctx[rs], vc_ref[0, :, ch]))
        o_ref[0, qoff:qoff + tq, cp] = _normalised_pair(outs[0], outs[1]).astype(BF16)


def _na_attention(q, k, v, kc, vc, tdup):
    b, s, w = q.shape
    lc = kc.shape[1]
    rows = s // GRID_W
    n_blocks = 4
    assert rows % (n_blocks * NA_QROWS) == 0
    tq = n_blocks * NA_QROWS * GRID_W
    body = functools.partial(_na_body, rows=rows, n_blocks=n_blocks)
    whole = lambda n, wd: pl.BlockSpec((1, n, wd), lambda bi, r: (bi, 0, 0))
    return pl.pallas_call(
        body,
        grid=(b, rows // (n_blocks * NA_QROWS)),
        in_specs=[pl.BlockSpec((1, tq, w), lambda bi, r: (bi, r, 0)),
                  whole(s, w), whole(s, 2 * w), whole(lc, w), whole(lc, 2 * w),
                  pl.BlockSpec(tdup.shape, lambda bi, r: (0, 0, 0, 0))],
        out_specs=pl.BlockSpec((1, tq, w), lambda bi, r: (bi, r, 0)),
        out_shape=jax.ShapeDtypeStruct((b, s, w), BF16),
        compiler_params=_cparams(("parallel", "arbitrary"), 48),
        name="na_attention",
    )(q, k, v, kc, vc, tdup)


def _ctx_attn_body(q_ref, k_ref, v_ref, o_ref):
    lo = lax.broadcasted_iota(jnp.int32, (q_ref.shape[1], LANES), 1) < HDIM
    for p in range(q_ref.shape[-1] // LANES):
        cp = slice(p * LANES, (p + 1) * LANES)
        q = q_ref[0, :, cp]
        k = k_ref[0, :, cp]
        outs = []
        for a in range(2):
            ch = slice((2 * p + a) * LANES, (2 * p + a + 1) * LANES)
            qa = jnp.where(lo if a == 0 else jnp.logical_not(lo), q, jnp.zeros_like(q))
            outs.append(_softmax_pv([_dot_t(qa, k)], [v_ref[0, :, ch]]))
        o_ref[0, :, cp] = _normalised_pair(outs[0], outs[1]).astype(BF16)


def _ctx_attention(q, k, v):
    b, l, w = q.shape
    spec = pl.BlockSpec((1, l, w), lambda bi: (bi, 0, 0))
    return pl.pallas_call(
        _ctx_attn_body,
        grid=(b,),
        in_specs=[spec, spec, pl.BlockSpec((1, l, 2 * w), lambda bi: (bi, 0, 0))],
        out_specs=spec,
        out_shape=jax.ShapeDtypeStruct((b, l, w), BF16),
        compiler_params=_cparams(("parallel",), 32),
        name="ctx_attention",
    )(q, k, v)


def _outproj_body(*refs, n_parts):
    y_refs = refs[:n_parts]
    w_refs = refs[n_parts:2 * n_parts]
    h_ref, gt_ref, g_ref, o_ref = refs[2 * n_parts:]
    y = _dot(y_refs[0][0], w_refs[0][...])
    for yr, wr in zip(y_refs[1:], w_refs[1:]):
        y = y + _dot(yr[0], wr[...])
    o_ref[0] = h_ref[0] + gt_ref[0] * _rms(y, g_ref[0])


def _outproj(ys, ws, h, gt, g_post):
    b, l, d = h.shape
    tm = _tile(l, 1024)
    row = lambda bi, i: (bi, i, 0)
    c2 = lambda bi, i: (0, 0)
    body = functools.partial(_outproj_body, n_parts=len(ys))
    return pl.pallas_call(
        body,
        grid=(b, l // tm),
        in_specs=([pl.BlockSpec((1, tm, y.shape[-1]), row) for y in ys]
                  + [pl.BlockSpec(w.shape, c2) for w in ws]
                  + [pl.BlockSpec((1, tm, d), row), _vec_spec(gt), _vec_spec(g_post)]),
        out_specs=pl.BlockSpec((1, tm, d), row),
        out_shape=jax.ShapeDtypeStruct((b, l, d), F32),
        compiler_params=_cparams(("parallel", "parallel"), 40),
        name="outproj",
    )(*ys, *ws, h, gt[0], g_post[0])


def _ffn_body(xp_ref, x_ref, xn_ref, sh_ref, sc_ref, gt_ref, gpre_ref, gpost_ref,
              wu_ref, cw_ref, cb_ref, wd_ref,
              o_ref, xs_ref, p3_ref, act_ref, acc_ref, *, tm, n_tiles, n_chunks, groups):
    i = pl.program_id(1)
    g = gpre_ref[0]
    sh = sh_ref[0]
    sc = sc_ref[0]
    nv = tm // SUBLANES
    nl = x_ref.shape[-1] // LANES
    xm = _modulate(x_ref[0], g, sh, sc)
    for j in range(nl):
        for s in range(SUBLANES):
            p3_ref[j, pl.ds(s, nv, stride=SUBLANES), :] = xm[s * nv:(s + 1) * nv, j * LANES:(j + 1) * LANES]
    xs_ref[0:tm] = jnp.concatenate([p3_ref[j] for j in range(nl)], axis=1).astype(BF16)
    rid = lax.broadcasted_iota(jnp.int32, (2 * SUBLANES, 1), 0)
    keep = jnp.where(rid == 0, (i > 0).astype(F32), jnp.where(rid == 1, (i < n_tiles - 1).astype(F32), 0.0))
    halo = jnp.concatenate([xp_ref[0, SUBLANES - 1:SUBLANES], xn_ref[0, 0:1],
                            jnp.zeros((2 * SUBLANES - 2, x_ref.shape[-1]), F32)], axis=0)
    xs_ref[tm:tm + 2 * SUBLANES] = (_modulate(halo, g, sh, sc) * keep).astype(BF16)

    srow = lax.broadcasted_iota(jnp.int32, (SUBLANES, FFN_CW), 0)

    def conv(hh, cw, cb):
        hm = hh[0:tm]
        first = jnp.where(srow == 0, hh[tm:tm + 1], pltpu.roll(hm[tm - SUBLANES:tm], 1, axis=0))
        last = jnp.where(srow == SUBLANES - 1, hh[tm + 1:tm + 2], pltpu.roll(hm[0:SUBLANES], SUBLANES - 1, axis=0))
        prev = jnp.concatenate([first, hm[0:tm - SUBLANES]], axis=0)
        nxt = jnp.concatenate([hm[SUBLANES:tm], last], axis=0)
        return prev * cw[0:1] + hm * cw[1:2] + nxt * cw[2:3] + cb

    ff = n_chunks * FFN_CW

    def up(c):
        xs = xs_ref[...]
        return (_dot(xs, wu_ref[:, c * FFN_CW:(c + 1) * FFN_CW]),
                _dot(xs, wu_ref[:, ff + c * FFN_CW:ff + (c + 1) * FFN_CW]))

    pending = up(0)
    k0 = 0
    for c in range(n_chunks):
        ha, hg = pending
        if c + 1 < n_chunks:
            pending = up(c + 1)
        cc = slice(c * FFN_CW, (c + 1) * FFN_CW)
        cg = slice(ff + c * FFN_CW, ff + (c + 1) * FFN_CW)
        a = conv(ha, cw_ref[:, cc], cb_ref[:, cc])
        gg = conv(hg, cw_ref[:, cg], cb_ref[:, cg])
        act_ref[:, cc] = (_silu(gg) * a).astype(BF16)
        if c + 1 in groups:
            k1 = (c + 1) * FFN_CW
            part = _dot(act_ref[:, k0:k1], wd_ref[k0:k1, :])
            if k0 == 0:
                acc_ref[...] = part
            else:
                acc_ref[...] += part
            k0 = k1
    f = gt_ref[0] * _rms(acc_ref[...], gpost_ref[0])
    for j in range(nl):
        p3_ref[j] = f[:, j * LANES:(j + 1) * LANES]
    for s in range(SUBLANES):
        fs = jnp.concatenate([p3_ref[j, pl.ds(s, nv, stride=SUBLANES), :] for j in range(nl)], axis=1)
        o_ref[0, s * nv:(s + 1) * nv, :] = x_ref[0, s * nv:(s + 1) * nv, :] + fs


def _ffn(h, sh, sc, gt, g_pre, g_post, wts):
    wu, cw, cb, wd = wts
    b, l, d = h.shape
    tm = _tile(l, 512)
    n_tiles = l // tm
    ff = wd.shape[0]
    n_chunks = ff // FFN_CW
    groups = (6, n_chunks)
    hb = tm // SUBLANES
    last_hb = l // SUBLANES - 1
    row = lambda bi, i: (bi, i, 0)
    c2 = lambda bi, i: (0, 0)
    whole = lambda a: pl.BlockSpec(a.shape, lambda bi, i: (0,) * a.ndim, pipeline_mode=pl.Buffered(1))
    body = functools.partial(_ffn_body, tm=tm, n_tiles=n_tiles, n_chunks=n_chunks, groups=groups)
    return pl.pallas_call(
        body,
        grid=(b, n_tiles),
        in_specs=[pl.BlockSpec((1, SUBLANES, d), lambda bi, i: (bi, jnp.maximum(i * hb - 1, 0), 0)),
                  pl.BlockSpec((1, tm, d), row),
                  pl.BlockSpec((1, SUBLANES, d), lambda bi, i: (bi, jnp.minimum((i + 1) * hb, last_hb), 0)),
                  _vec_spec(sh), _vec_spec(sc), _vec_spec(gt), _vec_spec(g_pre), _vec_spec(g_post),
                  whole(wu), whole(cw), whole(cb), whole(wd)],
        out_specs=pl.BlockSpec((1, tm, d), row),
        out_shape=jax.ShapeDtypeStruct((b, l, d), F32),
        scratch_shapes=[pltpu.VMEM((tm + 2 * SUBLANES, d), BF16), pltpu.VMEM((d // LANES, tm, LANES), F32),
                        pltpu.VMEM((tm, ff), BF16), pltpu.VMEM((tm, d), F32)],
        compiler_params=_cparams(("parallel", "parallel"), 56),
        name="conv_ffn",
    )(h, h, h, sh[0], sc[0], gt[0], g_pre[0], g_post[0], wu, cw, cb, wd)


def _ffn_weights(w_up, conv_w, conv_b, w_down):
    return w_up.astype(BF16), conv_w, conv_b[None, :], w_down.astype(BF16)


def _inproj_c_body(h_ref, sh_ref, sc_ref, g_ref, wq_ref, wk_ref, wv_ref, gq_ref, gk_ref, bd_ref,
                   cos_ref, sin_ref, *out_refs, n_qtiles, sub, n_sub):
    bd = bd_ref[...]
    if n_qtiles:
        q_ref, k_ref, v_ref = out_refs
    else:
        k_ref, v_ref = out_refs
    projs = []
    for t in range(n_sub):
        xm = _modulate(h_ref[0, t * sub:(t + 1) * sub], g_ref[0], sh_ref[0], sc_ref[0]).astype(BF16)
        tq = _dot(xm, wq_ref[...]) if n_qtiles else None
        projs.append((tq, _dot(xm, wk_ref[...]), _dot(xm, wv_ref[...])))
    for t in range(n_sub):
        rows = slice(t * sub, (t + 1) * sub)
        tq, tk, tv = projs[t]
        _store_ones_padded_t(v_ref, rows, tv)
        tiles = [(tq[:, 2 * LANES * j:2 * LANES * j + LANES], tq[:, 2 * LANES * j + LANES:2 * LANES * (j + 1)])
                 for j in range(n_qtiles)] + [(tk[:, :LANES], tk[:, LANES:])]
        sq = jnp.concatenate([(t1 * t1 + t2 * t2).astype(BF16) for t1, t2 in tiles], axis=0)
        ss = _dot(sq, bd)
        cs = cos_ref[rows]
        sn = sin_ref[rows]
        for j, (t1, t2) in enumerate(tiles):
            is_q = j < n_qtiles
            o_ref = q_ref if is_q else k_ref
            gains = gq_ref[...] if is_q else gk_ref[...]
            c0 = 2 * LANES * j if is_q else 0
            r = lax.rsqrt(ss[j * sub:(j + 1) * sub] * (1.0 / HDIM) + EPS)
            a1 = t1 * r * gains[0:1]
            a2 = t2 * r * gains[1:2]
            o_ref[0, rows, c0:c0 + LANES] = (a1 * cs - a2 * sn).astype(BF16)
            o_ref[0, rows, c0 + LANES:c0 + 2 * LANES] = (a1 * sn + a2 * cs).astype(BF16)


def _inproj_c(h, sh, sc, g_pre, wq, wk, wv, gq, gk, bd, cos, sin, with_q):
    b, l, d = h.shape
    sub = _tile(l, 512)
    n_sub = 2 if l % (2 * sub) == 0 else 1
    tm = sub * n_sub
    row = lambda bi, i: (bi, i, 0)
    c2 = lambda bi, i: (0, 0)
    nq, nkv = wq.shape[1], wk.shape[1]
    n_qtiles = nq // (2 * LANES) if with_q else 0
    body = functools.partial(_inproj_c_body, n_qtiles=n_qtiles, sub=sub, n_sub=n_sub)
    out_specs = [pl.BlockSpec((1, tm, nkv), row), pl.BlockSpec((1, 2 * nkv, tm), lambda bi, i: (bi, 0, i))]
    out_shape = [jax.ShapeDtypeStruct((b, l, nkv), BF16), jax.ShapeDtypeStruct((b, 2 * nkv, l), BF16)]
    if with_q:
        out_specs = [pl.BlockSpec((1, tm, nq), row)] + out_specs
        out_shape = [jax.ShapeDtypeStruct((b, l, nq), BF16)] + out_shape
    return pl.pallas_call(
        body,
        grid=(b, l // tm),
        in_specs=[pl.BlockSpec((1, tm, d), row), _vec_spec(sh), _vec_spec(sc), _vec_spec(g_pre),
                  pl.BlockSpec(wq.shape, c2), pl.BlockSpec(wk.shape, c2), pl.BlockSpec(wv.shape, c2),
                  pl.BlockSpec(gq.shape, c2), pl.BlockSpec(gk.shape, c2), pl.BlockSpec(bd.shape, c2),
                  pl.BlockSpec((tm, LANES), lambda bi, i: (i, 0)),
                  pl.BlockSpec((tm, LANES), lambda bi, i: (i, 0))],
        out_specs=out_specs,
        out_shape=out_shape,
        compiler_params=_cparams(("parallel", "parallel"), 40),
        name="inproj_c",
    )(h, sh[0], sc[0], g_pre[0], wq, wk, wv, gq, gk, bd, cos, sin)


def _gqa_body(q_ref, k_ref, vt_ref, kc_ref, vct_ref, o_ref, *, tq, n_sub):
    k = k_ref[0]
    kc = kc_ref[0]
    seg = (lax.broadcasted_iota(jnp.int32, (tq, 2 * LANES), 1) % LANES) // (HDIM // 2)

    def scores(t):
        q = q_ref[0, t * tq:(t + 1) * tq]
        zero = jnp.zeros_like(q)
        lhs = jnp.concatenate([jnp.where(seg == s, q, zero) for s in range(4)], axis=0)
        return _dot_t(k, lhs), _dot_t(kc, lhs)

    pending = scores(0)
    for t in range(n_sub):
        s_lat, s_ctx = pending
        if t + 1 < n_sub:
            pending = scores(t + 1)
        m = jnp.maximum(s_lat.max(axis=0, keepdims=True), s_ctx.max(axis=0, keepdims=True))
        p_lat = jnp.exp2(s_lat - m).astype(BF16)
        p_ctx = jnp.exp2(s_ctx - m).astype(BF16)
        outs = []
        for s in range(4):
            cq = slice(s * tq, (s + 1) * tq)
            rv = slice(s * LANES, (s + 1) * LANES)
            ot = _dot(vt_ref[0, rv, :], p_lat[:, cq]) + _dot(vct_ref[0, rv, :], p_ctx[:, cq])
            outs.append(ot.T)
        o_ref[0, t * tq:(t + 1) * tq] = jnp.concatenate(
            [_normalised_pair(outs[0], outs[1]), _normalised_pair(outs[2], outs[3])], axis=1).astype(BF16)


def _gqa_attention(q, k, vt, kc, vct):
    b, s, nq = q.shape
    lc = kc.shape[1]
    nkv = k.shape[-1]
    n_sub = 4
    tq = 256 * n_sub
    body = functools.partial(_gqa_body, tq=tq // n_sub, n_sub=n_sub)
    kv_spec = lambda n, w: pl.BlockSpec((1, n, w), lambda bi, i, j: (bi, 0, 0))
    return pl.pallas_call(
        body,
        grid=(b, s // tq, nq // (2 * LANES)),
        in_specs=[pl.BlockSpec((1, tq, 2 * LANES), lambda bi, i, j: (bi, i, j)),
                  kv_spec(s, nkv), kv_spec(2 * nkv, s), kv_spec(lc, nkv), kv_spec(2 * nkv, lc)],
        out_specs=pl.BlockSpec((1, tq, 2 * LANES), lambda bi, i, j: (bi, i, j)),
        out_shape=jax.ShapeDtypeStruct((b, s, nq), BF16),
        compiler_params=_cparams(("parallel", "arbitrary", "arbitrary"), 56),
        name="gqa_attention",
    )(q, k, vt, kc, vct)


def _na_bias_table(rpb):
    qc = np.arange(GRID_W)[:, None]
    kc = np.arange(GRID_W)[None, :]
    c0 = np.clip(qc - NA_COLS // 2, 0, GRID_W - NA_COLS)
    in_win = (kc >= c0) & (kc < c0 + NA_COLS)
    n_h, n_dr, n_dc = rpb.shape
    period = 2 * GRID_W
    lead = GRID_W - NA_COLS
    sig = jnp.pad(rpb, ((0, 0), (0, 0), (lead, period - n_dc - lead)))
    toep = jnp.tile(sig, (1, 1, GRID_W))[..., :GRID_W * (period - 1)].reshape(n_h, n_dr, GRID_W, period - 1)
    toep = toep[..., GRID_W - 1:]
    t = jnp.where(in_win, toep * LOG2E, NEG_INF)
    t = jnp.concatenate([t, jnp.full_like(t[:, :1], NEG_INF)], axis=1)
    return jnp.concatenate([t, t], axis=-1)


def _rope_tables(n_tokens):
    t = np.arange(n_tokens)
    quarter = HDIM // 4
    inv = (ROPE_THETA ** (-np.arange(quarter, dtype=np.float32) / quarter)).astype(np.float32)
    rows = (t // GRID_W).astype(np.float32)[:, None] * inv
    cols = (t % GRID_W).astype(np.float32)[:, None] * inv
    ang = np.tile(np.concatenate([rows, cols], axis=-1), (1, LANES // (HDIM // 2)))
    return jnp.asarray(np.cos(ang), F32), jnp.asarray(np.sin(ang), F32)


def kernel(x, c, ctx, c_ctx, w_ada, b_ada, norm_g, w_in_ab, a_w_s, a_b_s, a_v_g, b_rpb, w_out_ab,
           w_qkv_c, c_q_g, c_k_g, w_out_c, w_up, conv_w, conv_b, w_down):
    bsz, seq, d = x.shape
    lc = ctx.shape[1]
    depth = w_ada.shape[0]
    aw = a_v_g.shape[-1]
    n_heads_b = b_rpb.shape[1]
    bw = n_heads_b * HDIM
    n_kv = 4
    n_heads_c = w_out_c.shape[1] // HDIM
    assert seq % (NA_QROWS * GRID_W) == 0 and seq // GRID_W >= NA_KROWS
    assert aw // A_GROUPS == A_CHUNK and w_up.shape[-1] // 2 % FFN_CW == 0

    n_rows = 8 * ((bsz + 1 + 7) // 8)
    cv = jnp.concatenate([c, c_ctx[None], jnp.zeros((n_rows - bsz - 1, d), F32)], axis=0)
    mods = _ada(cv, w_ada, b_ada).reshape(depth * n_rows * 6, 1, d)
    gains = norm_g.reshape(depth * 4, 1, d)

    def mod_vec(layer, j, is_ctx):
        base = layer * n_rows * 6 + j
        return mods, ((lambda bi: base + bsz * 6) if is_ctx else (lambda bi: base + bi * 6))

    def gain_vec(layer, j):
        return gains, (lambda bi: layer * 4 + j)

    cos_l, sin_l = _rope_tables(seq)
    cos_c, sin_c = jnp.ones((lc, LANES), F32), jnp.zeros((lc, LANES), F32)

    h, hc = x, ctx
    for layer in range(depth):
        with_ctx = layer < depth - 1
        lat = [mod_vec(layer, j, False) for j in range(6)]
        cx = [mod_vec(layer, j, True) for j in range(6)]
        g_pre_m, g_post_m, g_pre_f, g_post_f = [gain_vec(layer, j) for j in range(4)]
        if layer % 2 == 0:
            e = layer // 2
            w_in = w_in_ab[e].astype(BF16)
            w_s = a_w_s[e].astype(BF16)
            b_s_b = jnp.broadcast_to(a_b_s[e][:, :, None], a_w_s[e].shape[:2] + (aw // A_GROUPS,))
            g_v = a_v_g[e][None, :]
            w_out = w_out_ab[e].astype(BF16)
            ya, q, k, v = _inproj_ab(h, lat[0], lat[1], g_pre_m, w_in, w_s, b_s_b, g_v, aw, bw)
            yca, qc, kc, vc = _inproj_ab(hc, cx[0], cx[1], g_pre_m, w_in, w_s, b_s_b, g_v, aw, bw)
            yb = _na_attention(q, k, v, kc, vc, _na_bias_table(b_rpb[e]))
            h = _outproj([ya, yb], [w_out[:aw], w_out[aw:]], h, lat[2], g_post_m)
            if with_ctx:
                ycb = _ctx_attention(qc, kc, vc)
                hc = _outproj([yca, ycb], [w_out[:aw], w_out[aw:]], hc, cx[2], g_post_m)
        else:
            o = layer // 2
            nq = n_heads_c * HDIM
            nkv = n_kv * HDIM
            per = n_heads_c // n_kv
            half = HDIM // 2
            wqkv = w_qkv_c[o].astype(BF16)
            wq = wqkv[:, :nq].reshape(d, n_kv, per, 2, half).transpose(0, 2, 3, 1, 4).reshape(d, nq)
            wk = wqkv[:, nq:nq + nkv].reshape(d, n_kv, 2, half).transpose(0, 2, 1, 3).reshape(d, nkv)
            wv = wqkv[:, nq + nkv:]
            reps = LANES // half
            gq = (jnp.stack([jnp.tile(c_q_g[o][:half], reps), jnp.tile(c_q_g[o][half:], reps)])
                  * (HDIM ** -0.5 * LOG2E))
            gk = jnp.stack([jnp.tile(c_k_g[o][:half], reps), jnp.tile(c_k_g[o][half:], reps)])
            seg = np.arange(LANES) // half
            bd = jnp.asarray(seg[:, None] == seg[None, :], BF16)
            w_out = w_out_c[o].astype(BF16).reshape(n_kv, per, HDIM, d).transpose(1, 0, 2, 3).reshape(nq, d)
            q, k, v = _inproj_c(h, lat[0], lat[1], g_pre_m, wq, wk, wv, gq, gk, bd, cos_l, sin_l, True)
            kc, vc = _inproj_c(hc, cx[0], cx[1], g_pre_m, wq, wk, wv, gq, gk, bd, cos_c, sin_c, False)
            y = _gqa_attention(q, k, v, kc, vc)
            h = _outproj([y], [w_out], h, lat[2], g_post_m)
            if with_ctx:
                raise NotImplementedError("context update after a grouped-query layer")
        wts = _ffn_weights(w_up[layer], conv_w[layer], conv_b[layer], w_down[layer])
        h = _ffn(h, lat[3], lat[4], lat[5], g_pre_f, g_post_f, wts)
        if with_ctx:
            hc = _ffn(hc, cx[3], cx[4], cx[5], g_pre_f, g_post_f, wts)
    return h
```

```python
import functools

import numpy as np
import jax
import jax.numpy as jnp
from jax import lax
from jax.experimental import pallas as pl
from jax.experimental.pallas import tpu as pltpu

F32 = jnp.float32
BF16 = jnp.bfloat16

EPS = 1e-6
NEG_INF = -1e30
GRID_W = 64
ROPE_THETA = 10000.0
LOG2E = 1.4426950408889634
A_CHUNK = 128
A_GROUPS = 4
NA_ROWS = 8
NA_COLS = 16
HDIM = 64
LANES = 128
NA_QROWS = 4
NA_KROWS = 12
SUBLANES = 8
FFN_CW = 256
V7X_VMEM_BYTES = 64 * 1024 * 1024


def _cparams(sem, vmem_mb):
    assert vmem_mb * 1024 * 1024 < V7X_VMEM_BYTES
    return pltpu.CompilerParams(dimension_semantics=sem, vmem_limit_bytes=vmem_mb * 1024 * 1024)


def _tile(n, pref):
    return pref if n % pref == 0 else n


def _vec_spec(vec):
    table, row = vec
    return pl.BlockSpec((1, 1, table.shape[-1]), lambda bi, *_: (row(bi), 0, 0))


def _rms(x, g):
    return x * lax.rsqrt(jnp.mean(x * x, axis=-1, keepdims=True) + EPS) * g


def _modulate(x, g, shift, scale):
    return _rms(x, g) * (1.0 + scale) + shift


def _gelu(x):
    return 0.5 * x * (1.0 + jnp.tanh(0.7978845608028654 * (x + 0.044715 * (x * x * x))))


def _silu(x):
    return x * jax.nn.sigmoid(x)


def _dot(a, b):
    return jnp.dot(a, b, preferred_element_type=F32)


def _dot_t(a, b):
    return lax.dot_general(a, b, (((1,), (1,)), ((), ())), preferred_element_type=F32)


def _ada_body(c_ref, w_ref, b_ref, o_ref):
    s = _silu(c_ref[...])
    o_ref[0] = jnp.dot(s, w_ref[0], preferred_element_type=F32,
                       precision=lax.Precision.HIGHEST) + b_ref[0]


def _ada(cv, w_ada, b_ada):
    depth, d, n = w_ada.shape
    rows = cv.shape[0]
    tn = _tile(n, 1536)
    return pl.pallas_call(
        _ada_body,
        grid=(depth, n // tn),
        in_specs=[pl.BlockSpec((rows, d), lambda l, j: (0, 0)),
                  pl.BlockSpec((1, d, tn), lambda l, j: (l, 0, j)),
                  pl.BlockSpec((1, 1, tn), lambda l, j: (l, 0, j))],
        out_specs=pl.BlockSpec((1, rows, tn), lambda l, j: (l, 0, j)),
        out_shape=jax.ShapeDtypeStruct((depth, rows, n), F32),
        compiler_params=_cparams(("parallel", "parallel"), 40),
        name="ada",
    )(cv, w_ada, b_ada.reshape(depth, 1, n))


def _store_ones_padded(v_ref, rows, vals):
    lo = lax.broadcasted_iota(jnp.int32, (vals.shape[0], LANES), 1) < HDIM
    for t in range(vals.shape[1] // LANES):
        vt = vals[:, t * LANES:(t + 1) * LANES]
        v_ref[0, rows, (2 * t) * LANES:(2 * t + 1) * LANES] = jnp.where(lo, vt, 1.0).astype(BF16)
        v_ref[0, rows, (2 * t + 1) * LANES:(2 * t + 2) * LANES] = jnp.where(lo, 1.0, vt).astype(BF16)


def _inproj_ab_body(h_ref, sh_ref, sc_ref, g_ref, w_ref, ws_ref, bs_ref, gv_ref,
                    ya_ref, q_ref, k_ref, v_ref, *, sub, n_sub, aw):
    bw = q_ref.shape[-1]
    gd = aw // A_GROUPS
    bounds = [0, aw, 2 * aw, 2 * aw + bw, 2 * aw + 2 * bw, 2 * aw + 3 * bw]
    projs = []
    for t in range(n_sub):
        xm = _modulate(h_ref[0, t * sub:(t + 1) * sub], g_ref[0], sh_ref[0], sc_ref[0]).astype(BF16)
        projs.append([_dot(xm, w_ref[:, c0:c1]) for c0, c1 in zip(bounds[:-1], bounds[1:])])
    for t in range(n_sub):
        rows = slice(t * sub, (t + 1) * sub)
        tu, tva, tq, tk, tv = projs[t]
        q_ref[0, rows] = (tq * (HDIM ** -0.5 * LOG2E)).astype(BF16)
        k_ref[0, rows] = tk.astype(BF16)
        _store_ones_padded(v_ref, rows, tv)
        u = _gelu(tu)
        va = _gelu(tva)
        mu = jnp.mean(va, axis=-1, keepdims=True)
        d = va - mu
        var = jnp.mean(d * d, axis=-1, keepdims=True)
        vn = (d * lax.rsqrt(var + EPS) * gv_ref[...]).astype(BF16)
        for n in range(sub // A_CHUNK):
            r0, r1 = n * A_CHUNK, (n + 1) * A_CHUNK
            for g in range(A_GROUPS):
                c0, c1 = g * gd, (g + 1) * gd
                s = _dot(ws_ref[g], vn[r0:r1, c0:c1]) + bs_ref[g]
                ya_ref[0, t * sub + r0:t * sub + r1, c0:c1] = (u[r0:r1, c0:c1] * s).astype(BF16)


def _inproj_ab(h, sh, sc, g_pre, w_in, w_s, b_s_b, g_v, aw, bw):
    b, l, d = h.shape
    sub = _tile(l, 512)
    n_sub = 2 if l % (2 * sub) == 0 else 1
    tm = sub * n_sub
    n_in = w_in.shape[1]
    row = lambda bi, i: (bi, i, 0)
    c2 = lambda bi, i: (0, 0)
    c3 = lambda bi, i: (0, 0, 0)
    body = functools.partial(_inproj_ab_body, sub=sub, n_sub=n_sub, aw=aw)
    widths = (aw, bw, bw, 2 * bw)
    return pl.pallas_call(
        body,
        grid=(b, l // tm),
        in_specs=[pl.BlockSpec((1, tm, d), row), _vec_spec(sh), _vec_spec(sc), _vec_spec(g_pre),
                  pl.BlockSpec((d, n_in), c2),
                  pl.BlockSpec(w_s.shape, c3), pl.BlockSpec(b_s_b.shape, c3),
                  pl.BlockSpec((1, aw), c2)],
        out_specs=[pl.BlockSpec((1, tm, w), row) for w in widths],
        out_shape=[jax.ShapeDtypeStruct((b, l, w), BF16) for w in widths],
        compiler_params=_cparams(("parallel", "parallel"), 48),
        name="inproj_ab",
    )(h, sh[0], sc[0], g_pre[0], w_in, w_s, b_s_b, g_v)


def _softmax_pv(scores, values):
    m = scores[0].max(axis=-1, keepdims=True)
    for s in scores[1:]:
        m = jnp.maximum(m, s.max(axis=-1, keepdims=True))
    o = _dot(jnp.exp2(scores[0] - m).astype(BF16), values[0])
    for s, v in zip(scores[1:], values[1:]):
        o = o + _dot(jnp.exp2(s - m).astype(BF16), v)
    return o


def _normalised_pair(o_even, o_odd):
    lo = lax.broadcasted_iota(jnp.int32, o_even.shape, 1) < HDIM
    return jnp.where(lo, o_even / pltpu.roll(o_even, HDIM, axis=1), o_odd / pltpu.roll(o_odd, HDIM, axis=1))


def _na_body(q_ref, k_ref, v_ref, kc_ref, vc_ref, t_ref, o_ref, *, rows, n_blocks):
    for u in range(n_blocks):
        _na_block(q_ref, k_ref, v_ref, kc_ref, vc_ref, t_ref, o_ref,
                  n_blocks * pl.program_id(1) + u, u * NA_QROWS * GRID_W, rows)


def _na_block(q_ref, k_ref, v_ref, kc_ref, vc_ref, t_ref, o_ref, rb, qoff, rows):
    tq = NA_QROWS * GRID_W
    nk = NA_KROWS * GRID_W
    start = jnp.clip(NA_QROWS * rb - NA_ROWS // 2, 0, rows - NA_KROWS)
    koff = pl.multiple_of(start * GRID_W, NA_QROWS * GRID_W)
    lo_q = lax.broadcasted_iota(jnp.int32, (tq, LANES), 1) < HDIM
    lo_b = lax.broadcasted_iota(jnp.int32, (GRID_W, LANES), 1) < GRID_W
    idx = []
    for ri in range(NA_QROWS):
        r = NA_QROWS * rb + ri
        r0 = jnp.clip(r - NA_ROWS // 2, 0, rows - NA_ROWS)
        row = []
        for j in range(NA_KROWS):
            kr = start + j
            valid = jnp.logical_and(kr >= r0, kr < r0 + NA_ROWS)
            row.append(jnp.where(valid, kr - r + NA_ROWS - 1, 2 * NA_ROWS - 1))
        idx.append(row)
    for p in range(q_ref.shape[-1] // LANES):
        cp = slice(p * LANES, (p + 1) * LANES)
        q = q_ref[0, qoff:qoff + tq, cp]
        zero = jnp.zeros_like(q)
        lhs = jnp.concatenate([jnp.where(lo_q, q, zero), jnp.where(lo_q, zero, q)], axis=0)
        bias = jnp.concatenate([
            jnp.concatenate([
                jnp.where(lo_b, t_ref[2 * p + a, idx[ri][2 * m]], t_ref[2 * p + a, idx[ri][2 * m + 1]])
                for m in range(NA_KROWS // 2)], axis=1)
            for a in range(2) for ri in range(NA_QROWS)], axis=0)
        s_loc = _dot_t(lhs, k_ref[0, pl.ds(koff, nk), cp]) + bias
        s_ctx = _dot_t(lhs, kc_ref[0, :, cp])
        m = jnp.maximum(s_loc.max(axis=-1, keepdims=True), s_ctx.max(axis=-1, keepdims=True))
        p_loc = jnp.exp2(s_loc - m).astype(BF16)
        p_ctx = jnp.exp2(s_ctx - m).astype(BF16)
        outs = []
        for a in range(2):
            ch = slice((2 * p + a) * LANES, (2 * p + a + 1) * LANES)
            rs = slice(a * tq, (a + 1) * tq)
            outs.append(_dot(p_loc[rs], v_ref[0, pl.ds(koff, nk), ch]) + _dot(p_ctx[rs], vc_ref[0, :, ch]))
        o_ref[0, qoff:qoff + tq, cp] = _normalised_pair(outs[0], outs[1]).astype(BF16)


def _na_attention(q, k, v, kc, vc, tdup):
    b, s, w = q.shape
    lc = kc.shape[1]
    rows = s // GRID_W
    n_blocks = 4
    assert rows % (n_blocks * NA_QROWS) == 0
    tq = n_blocks * NA_QROWS * GRID_W
    body = functools.partial(_na_body, rows=rows, n_blocks=n_blocks)
    whole = lambda n, wd: pl.BlockSpec((1, n, wd), lambda bi, r: (bi, 0, 0))
    return pl.pallas_call(
        body,
        grid=(b, rows // (n_blocks * NA_QROWS)),
        in_specs=[pl.BlockSpec((1, tq, w), lambda bi, r: (bi, r, 0)),
                  whole(s, w), whole(s, 2 * w), whole(lc, w), whole(lc, 2 * w),
                  pl.BlockSpec(tdup.shape, lambda bi, r: (0, 0, 0, 0))],
        out_specs=pl.BlockSpec((1, tq, w), lambda bi, r: (bi, r, 0)),
        out_shape=jax.ShapeDtypeStruct((b, s, w), BF16),
        compiler_params=_cparams(("parallel", "arbitrary"), 48),
        name="na_attention",
    )(q, k, v, kc, vc, tdup)


def _ctx_attn_body(q_ref, k_ref, v_ref, o_ref):
    lo = lax.broadcasted_iota(jnp.int32, (q_ref.shape[1], LANES), 1) < HDIM
    for p in range(q_ref.shape[-1] // LANES):
        cp = slice(p * LANES, (p + 1) * LANES)
        q = q_ref[0, :, cp]
        k = k_ref[0, :, cp]
        outs = []
        for a in range(2):
            ch = slice((2 * p + a) * LANES, (2 * p + a + 1) * LANES)
            qa = jnp.where(lo if a == 0 else jnp.logical_not(lo), q, jnp.zeros_like(q))
            outs.append(_softmax_pv([_dot_t(qa, k)], [v_ref[0, :, ch]]))
        o_ref[0, :, cp] = _normalised_pair(outs[0], outs[1]).astype(BF16)


def _ctx_attention(q, k, v):
    b, l, w = q.shape
    spec = pl.BlockSpec((1, l, w), lambda bi: (bi, 0, 0))
    return pl.pallas_call(
        _ctx_attn_body,
        grid=(b,),
        in_specs=[spec, spec, pl.BlockSpec((1, l, 2 * w), lambda bi: (bi, 0, 0))],
        out_specs=spec,
        out_shape=jax.ShapeDtypeStruct((b, l, w), BF16),
        compiler_params=_cparams(("parallel",), 32),
        name="ctx_attention",
    )(q, k, v)


def _outproj_body(*refs, n_parts):
    y_refs = refs[:n_parts]
    w_refs = refs[n_parts:2 * n_parts]
    h_ref, gt_ref, g_ref, o_ref = refs[2 * n_parts:]
    y = _dot(y_refs[0][0], w_refs[0][...])
    for yr, wr in zip(y_refs[1:], w_refs[1:]):
        y = y + _dot(yr[0], wr[...])
    o_ref[0] = h_ref[0] + gt_ref[0] * _rms(y, g_ref[0])


def _outproj(ys, ws, h, gt, g_post):
    b, l, d = h.shape
    tm = _tile(l, 1024)
    row = lambda bi, i: (bi, i, 0)
    c2 = lambda bi, i: (0, 0)
    body = functools.partial(_outproj_body, n_parts=len(ys))
    return pl.pallas_call(
        body,
        grid=(b, l // tm),
        in_specs=([pl.BlockSpec((1, tm, y.shape[-1]), row) for y in ys]
                  + [pl.BlockSpec(w.shape, c2) for w in ws]
                  + [pl.BlockSpec((1, tm, d), row), _vec_spec(gt), _vec_spec(g_post)]),
        out_specs=pl.BlockSpec((1, tm, d), row),
        out_shape=jax.ShapeDtypeStruct((b, l, d), F32),
        compiler_params=_cparams(("parallel", "parallel"), 40),
        name="outproj",
    )(*ys, *ws, h, gt[0], g_post[0])


def _ffn_body(xp_ref, x_ref, xn_ref, sh_ref, sc_ref, gt_ref, gpre_ref, gpost_ref,
              wu_ref, cw_ref, cb_ref, wd_ref,
              o_ref, xs_ref, p3_ref, act_ref, acc_ref, *, tm, n_tiles, n_chunks, groups, seq):
    i = pl.program_id(1)
    g = gpre_ref[0]
    sh = sh_ref[0]
    sc = sc_ref[0]
    nv = tm // SUBLANES
    nl = x_ref.shape[-1] // LANES
    xm = _modulate(x_ref[0], g, sh, sc)
    for j in range(nl):
        for s in range(SUBLANES):
            p3_ref[j, pl.ds(s, nv, stride=SUBLANES), :] = xm[s * nv:(s + 1) * nv, j * LANES:(j + 1) * LANES]
    xs_ref[0:tm] = jnp.concatenate([p3_ref[j] for j in range(nl)], axis=1).astype(BF16)
    rid = lax.broadcasted_iota(jnp.int32, (2 * SUBLANES, 1), 0)
    keep = jnp.where(rid == 0, (i > 0).astype(F32), jnp.where(rid == 1, (i < n_tiles - 1).astype(F32), 0.0))
    halo = jnp.concatenate([xp_ref[0, SUBLANES - 1:SUBLANES], xn_ref[0, 0:1],
                            jnp.zeros((2 * SUBLANES - 2, x_ref.shape[-1]), F32)], axis=0)
    xs_ref[tm:tm + 2 * SUBLANES] = (_modulate(halo, g, sh, sc) * keep).astype(BF16)

    srow = lax.broadcasted_iota(jnp.int32, (SUBLANES, FFN_CW), 0)
    inner_start = functools.reduce(jnp.logical_or, [srow == s for s in range(1, SUBLANES) if (s * nv) % seq == 0],
                                   srow < 0)
    inner_end = functools.reduce(jnp.logical_or,
                                 [srow == s for s in range(SUBLANES - 1) if ((s + 1) * nv) % seq == 0], srow < 0)

    def conv(hh, cw, cb):
        hm = hh[0:tm]
        first = jnp.where(srow == 0, hh[tm:tm + 1], pltpu.roll(hm[tm - SUBLANES:tm], 1, axis=0))
        last = jnp.where(srow == SUBLANES - 1, hh[tm + 1:tm + 2], pltpu.roll(hm[0:SUBLANES], SUBLANES - 1, axis=0))
        first = jnp.where(inner_start, 0.0, first)
        last = jnp.where(inner_end, 0.0, last)
        prev = jnp.concatenate([first, hm[0:tm - SUBLANES]], axis=0)
        nxt = jnp.concatenate([hm[SUBLANES:tm], last], axis=0)
        return prev * cw[0:1] + hm * cw[1:2] + nxt * cw[2:3] + cb

    ff = n_chunks * FFN_CW

    def up(c):
        xs = xs_ref[...]
        return (_dot(xs, wu_ref[:, c * FFN_CW:(c + 1) * FFN_CW]),
                _dot(xs, wu_ref[:, ff + c * FFN_CW:ff + (c + 1) * FFN_CW]))

    pending = up(0)
    k0 = 0
    for c in range(n_chunks):
        ha, hg = pending
        if c + 1 < n_chunks:
            pending = up(c + 1)
        cc = slice(c * FFN_CW, (c + 1) * FFN_CW)
        cg = slice(ff + c * FFN_CW, ff + (c + 1) * FFN_CW)
        a = conv(ha, cw_ref[:, cc], cb_ref[:, cc])
        gg = conv(hg, cw_ref[:, cg], cb_ref[:, cg])
        act_ref[:, cc] = (_silu(gg) * a).astype(BF16)
        if c + 1 in groups:
            k1 = (c + 1) * FFN_CW
            part = _dot(act_ref[:, k0:k1], wd_ref[k0:k1, :])
            if k0 == 0:
                acc_ref[...] = part
            else:
                acc_ref[...] += part
            k0 = k1
    f = gt_ref[0] * _rms(acc_ref[...], gpost_ref[0])
    for j in range(nl):
        p3_ref[j] = f[:, j * LANES:(j + 1) * LANES]
    for s in range(SUBLANES):
        fs = jnp.concatenate([p3_ref[j, pl.ds(s, nv, stride=SUBLANES), :] for j in range(nl)], axis=1)
        o_ref[0, s * nv:(s + 1) * nv, :] = x_ref[0, s * nv:(s + 1) * nv, :] + fs


def _ffn(h, sh, sc, gt, g_pre, g_post, wts, seq_len=None):
    wu, cw, cb, wd = wts
    b, l, d = h.shape
    tm = _tile(l, 512)
    n_tiles = l // tm
    seq = l if seq_len is None else seq_len
    assert seq == l or (l == tm and l % seq == 0 and seq % (tm // SUBLANES) == 0)
    ff = wd.shape[0]
    n_chunks = ff // FFN_CW
    groups = (6, n_chunks)
    hb = tm // SUBLANES
    last_hb = l // SUBLANES - 1
    row = lambda bi, i: (bi, i, 0)
    c2 = lambda bi, i: (0, 0)
    whole = lambda a: pl.BlockSpec(a.shape, lambda bi, i: (0,) * a.ndim, pipeline_mode=pl.Buffered(1))
    body = functools.partial(_ffn_body, tm=tm, n_tiles=n_tiles, n_chunks=n_chunks, groups=groups, seq=seq)
    return pl.pallas_call(
        body,
        grid=(b, n_tiles),
        in_specs=[pl.BlockSpec((1, SUBLANES, d), lambda bi, i: (bi, jnp.maximum(i * hb - 1, 0), 0)),
                  pl.BlockSpec((1, tm, d), row),
                  pl.BlockSpec((1, SUBLANES, d), lambda bi, i: (bi, jnp.minimum((i + 1) * hb, last_hb), 0)),
                  _vec_spec(sh), _vec_spec(sc), _vec_spec(gt), _vec_spec(g_pre), _vec_spec(g_post),
                  whole(wu), whole(cw), whole(cb), whole(wd)],
        out_specs=pl.BlockSpec((1, tm, d), row),
        out_shape=jax.ShapeDtypeStruct((b, l, d), F32),
        scratch_shapes=[pltpu.VMEM((tm + 2 * SUBLANES, d), BF16), pltpu.VMEM((d // LANES, tm, LANES), F32),
                        pltpu.VMEM((tm, ff), BF16), pltpu.VMEM((tm, d), F32)],
        compiler_params=_cparams(("parallel", "parallel"), 56),
        name="conv_ffn",
    )(h, h, h, sh[0], sc[0], gt[0], g_pre[0], g_post[0], wu, cw, cb, wd)


def _ffn_weights(w_up, conv_w, conv_b, w_down):
    return w_up.astype(BF16), conv_w, conv_b[None, :], w_down.astype(BF16)


def _inproj_c_body(h_ref, sh_ref, sc_ref, g_ref, wq_ref, wk_ref, wv_ref, gq_ref, gk_ref, bd_ref,
                   cos_ref, sin_ref, *out_refs, n_qtiles, sub, n_sub):
    bd = bd_ref[...]
    if n_qtiles:
        q_ref, k_ref, v_ref = out_refs
    else:
        k_ref, v_ref = out_refs
    projs = []
    for t in range(n_sub):
        xm = _modulate(h_ref[0, t * sub:(t + 1) * sub], g_ref[0], sh_ref[0], sc_ref[0]).astype(BF16)
        tq = _dot(xm, wq_ref[...]) if n_qtiles else None
        projs.append((tq, _dot(xm, wk_ref[...]), _dot(xm, wv_ref[...])))
    for t in range(n_sub):
        rows = slice(t * sub, (t + 1) * sub)
        tq, tk, tv = projs[t]
        _store_ones_padded(v_ref, rows, tv)
        tiles = [(tq[:, 2 * LANES * j:2 * LANES * j + LANES], tq[:, 2 * LANES * j + LANES:2 * LANES * (j + 1)])
                 for j in range(n_qtiles)] + [(tk[:, :LANES], tk[:, LANES:])]
        sq = jnp.concatenate([(t1 * t1 + t2 * t2).astype(BF16) for t1, t2 in tiles], axis=0)
        ss = _dot(sq, bd)
        cs = cos_ref[rows]
        sn = sin_ref[rows]
        for j, (t1, t2) in enumerate(tiles):
            is_q = j < n_qtiles
            o_ref = q_ref if is_q else k_ref
            gains = gq_ref[...] if is_q else gk_ref[...]
            c0 = 2 * LANES * j if is_q else 0
            r = lax.rsqrt(ss[j * sub:(j + 1) * sub] * (1.0 / HDIM) + EPS)
            a1 = t1 * r * gains[0:1]
            a2 = t2 * r * gains[1:2]
            o_ref[0, rows, c0:c0 + LANES] = (a1 * cs - a2 * sn).astype(BF16)
            o_ref[0, rows, c0 + LANES:c0 + 2 * LANES] = (a1 * sn + a2 * cs).astype(BF16)


def _inproj_c(h, sh, sc, g_pre, wq, wk, wv, gq, gk, bd, cos, sin, with_q):
    b, l, d = h.shape
    sub = _tile(l, 512)
    n_sub = 2 if l % (2 * sub) == 0 else 1
    tm = sub * n_sub
    row = lambda bi, i: (bi, i, 0)
    c2 = lambda bi, i: (0, 0)
    nq, nkv = wq.shape[1], wk.shape[1]
    n_qtiles = nq // (2 * LANES) if with_q else 0
    body = functools.partial(_inproj_c_body, n_qtiles=n_qtiles, sub=sub, n_sub=n_sub)
    out_specs = [pl.BlockSpec((1, tm, nkv), row), pl.BlockSpec((1, tm, 2 * nkv), row)]
    out_shape = [jax.ShapeDtypeStruct((b, l, nkv), BF16), jax.ShapeDtypeStruct((b, l, 2 * nkv), BF16)]
    if with_q:
        out_specs = [pl.BlockSpec((1, tm, nq), row)] + out_specs
        out_shape = [jax.ShapeDtypeStruct((b, l, nq), BF16)] + out_shape
    return pl.pallas_call(
        body,
        grid=(b, l // tm),
        in_specs=[pl.BlockSpec((1, tm, d), row), _vec_spec(sh), _vec_spec(sc), _vec_spec(g_pre),
                  pl.BlockSpec(wq.shape, c2), pl.BlockSpec(wk.shape, c2), pl.BlockSpec(wv.shape, c2),
                  pl.BlockSpec(gq.shape, c2), pl.BlockSpec(gk.shape, c2), pl.BlockSpec(bd.shape, c2),
                  pl.BlockSpec((tm, LANES), lambda bi, i: (i, 0)),
                  pl.BlockSpec((tm, LANES), lambda bi, i: (i, 0))],
        out_specs=out_specs,
        out_shape=out_shape,
        compiler_params=_cparams(("parallel", "parallel"), 40),
        name="inproj_c",
    )(h, sh[0], sc[0], g_pre[0], wq, wk, wv, gq, gk, bd, cos, sin)


def _gqa_body(q_ref, k_ref, v_ref, kc_ref, vc_ref, o_ref, *, tq, n_sub):
    k = k_ref[0]
    kc = kc_ref[0]
    seg = (lax.broadcasted_iota(jnp.int32, (tq, 2 * LANES), 1) % LANES) // (HDIM // 2)

    def scores(t):
        q = q_ref[0, t * tq:(t + 1) * tq]
        zero = jnp.zeros_like(q)
        lhs = jnp.concatenate([jnp.where(seg == s, q, zero) for s in range(4)], axis=0)
        return _dot_t(lhs, k), _dot_t(lhs, kc)

    pending = scores(0)
    for t in range(n_sub):
        s_lat, s_ctx = pending
        if t + 1 < n_sub:
            pending = scores(t + 1)
        m = jnp.maximum(s_lat.max(axis=-1, keepdims=True), s_ctx.max(axis=-1, keepdims=True))
        p_lat = jnp.exp2(s_lat - m).astype(BF16)
        p_ctx = jnp.exp2(s_ctx - m).astype(BF16)
        outs = []
        for s in range(4):
            rs = slice(s * tq, (s + 1) * tq)
            cs = slice(s * LANES, (s + 1) * LANES)
            outs.append(_dot(p_lat[rs], v_ref[0, :, cs]) + _dot(p_ctx[rs], vc_ref[0, :, cs]))
        o_ref[0, t * tq:(t + 1) * tq] = jnp.concatenate(
            [_normalised_pair(outs[0], outs[1]), _normalised_pair(outs[2], outs[3])], axis=1).astype(BF16)


def _gqa_attention(q, k, v, kc, vc):
    b, s, nq = q.shape
    lc = kc.shape[1]
    nkv = k.shape[-1]
    n_sub = 4
    tq = 256 * n_sub
    body = functools.partial(_gqa_body, tq=tq // n_sub, n_sub=n_sub)
    kv_spec = lambda n, w: pl.BlockSpec((1, n, w), lambda bi, i, j: (bi, 0, 0))
    return pl.pallas_call(
        body,
        grid=(b, s // tq, nq // (2 * LANES)),
        in_specs=[pl.BlockSpec((1, tq, 2 * LANES), lambda bi, i, j: (bi, i, j)),
                  kv_spec(s, nkv), kv_spec(s, 2 * nkv), kv_spec(lc, nkv), kv_spec(lc, 2 * nkv)],
        out_specs=pl.BlockSpec((1, tq, 2 * LANES), lambda bi, i, j: (bi, i, j)),
        out_shape=jax.ShapeDtypeStruct((b, s, nq), BF16),
        compiler_params=_cparams(("parallel", "arbitrary", "arbitrary"), 56),
        name="gqa_attention",
    )(q, k, v, kc, vc)


def _na_bias_table(rpb):
    qc = np.arange(GRID_W)[:, None]
    kc = np.arange(GRID_W)[None, :]
    c0 = np.clip(qc - NA_COLS // 2, 0, GRID_W - NA_COLS)
    in_win = (kc >= c0) & (kc < c0 + NA_COLS)
    n_h, n_dr, n_dc = rpb.shape
    period = 2 * GRID_W
    lead = GRID_W - NA_COLS
    sig = jnp.pad(rpb, ((0, 0), (0, 0), (lead, period - n_dc - lead)))
    toep = jnp.tile(sig, (1, 1, GRID_W))[..., :GRID_W * (period - 1)].reshape(n_h, n_dr, GRID_W, period - 1)
    toep = toep[..., GRID_W - 1:]
    t = jnp.where(in_win, toep * LOG2E, NEG_INF)
    t = jnp.concatenate([t, jnp.full_like(t[:, :1], NEG_INF)], axis=1)
    return jnp.concatenate([t, t], axis=-1)


def _rope_tables(n_tokens):
    t = np.arange(n_tokens)
    quarter = HDIM // 4
    inv = (ROPE_THETA ** (-np.arange(quarter, dtype=np.float32) / quarter)).astype(np.float32)
    rows = (t // GRID_W).astype(np.float32)[:, None] * inv
    cols = (t % GRID_W).astype(np.float32)[:, None] * inv
    ang = np.tile(np.concatenate([rows, cols], axis=-1), (1, LANES // (HDIM // 2)))
    return jnp.asarray(np.cos(ang), F32), jnp.asarray(np.sin(ang), F32)


def kernel(x, c, ctx, c_ctx, w_ada, b_ada, norm_g, w_in_ab, a_w_s, a_b_s, a_v_g, b_rpb, w_out_ab,
           w_qkv_c, c_q_g, c_k_g, w_out_c, w_up, conv_w, conv_b, w_down):
    bsz, seq, d = x.shape
    lc = ctx.shape[1]
    depth = w_ada.shape[0]
    aw = a_v_g.shape[-1]
    n_heads_b = b_rpb.shape[1]
    bw = n_heads_b * HDIM
    n_kv = 4
    n_heads_c = w_out_c.shape[1] // HDIM
    assert seq % (NA_QROWS * GRID_W) == 0 and seq // GRID_W >= NA_KROWS
    assert aw // A_GROUPS == A_CHUNK and w_up.shape[-1] // 2 % FFN_CW == 0

    n_rows = 8 * ((bsz + 1 + 7) // 8)
    cv = jnp.concatenate([c, c_ctx[None], jnp.zeros((n_rows - bsz - 1, d), F32)], axis=0)
    mods = _ada(cv, w_ada, b_ada).reshape(depth * n_rows * 6, 1, d)
    gains = norm_g.reshape(depth * 4, 1, d)

    def mod_vec(layer, j, is_ctx):
        base = layer * n_rows * 6 + j
        return mods, ((lambda bi: base + bsz * 6) if is_ctx else (lambda bi: base + bi * 6))

    def gain_vec(layer, j):
        return gains, (lambda bi: layer * 4 + j)

    merge = 2 if bsz % 2 == 0 and lc % A_CHUNK == 0 else 1
    merged = lambda a: a.reshape(bsz // merge, merge * lc, a.shape[-1])
    per_batch = lambda a: a.reshape(bsz, lc, a.shape[-1])

    cos_l, sin_l = _rope_tables(seq)
    cos_c, sin_c = jnp.ones((merge * lc, LANES), F32), jnp.zeros((merge * lc, LANES), F32)

    h, hc = x, merged(ctx)
    for layer in range(depth):
        with_ctx = layer < depth - 1
        lat = [mod_vec(layer, j, False) for j in range(6)]
        cx = [mod_vec(layer, j, True) for j in range(6)]
        g_pre_m, g_post_m, g_pre_f, g_post_f = [gain_vec(layer, j) for j in range(4)]
        if layer % 2 == 0:
            e = layer // 2
            w_in = w_in_ab[e].astype(BF16)
            w_s = a_w_s[e].astype(BF16)
            b_s_b = jnp.broadcast_to(a_b_s[e][:, :, None], a_w_s[e].shape[:2] + (aw // A_GROUPS,))
            g_v = a_v_g[e][None, :]
            w_out = w_out_ab[e].astype(BF16)
            ya, q, k, v = _inproj_ab(h, lat[0], lat[1], g_pre_m, w_in, w_s, b_s_b, g_v, aw, bw)
            yca, qc, kc, vc = _inproj_ab(hc, cx[0], cx[1], g_pre_m, w_in, w_s, b_s_b, g_v, aw, bw)
            qc, kc, vc = per_batch(qc), per_batch(kc), per_batch(vc)
            yb = _na_attention(q, k, v, kc, vc, _na_bias_table(b_rpb[e]))
            h = _outproj([ya, yb], [w_out[:aw], w_out[aw:]], h, lat[2], g_post_m)
            if with_ctx:
                ycb = merged(_ctx_attention(qc, kc, vc))
                hc = _outproj([yca, ycb], [w_out[:aw], w_out[aw:]], hc, cx[2], g_post_m)
        else:
            o = layer // 2
            nq = n_heads_c * HDIM
            nkv = n_kv * HDIM
            per = n_heads_c // n_kv
            half = HDIM // 2
            wqkv = w_qkv_c[o].astype(BF16)
            wq = wqkv[:, :nq].reshape(d, n_kv, per, 2, half).transpose(0, 2, 3, 1, 4).reshape(d, nq)
            wk = wqkv[:, nq:nq + nkv].reshape(d, n_kv, 2, half).transpose(0, 2, 1, 3).reshape(d, nkv)
            wv = wqkv[:, nq + nkv:]
            reps = LANES // half
            gq = (jnp.stack([jnp.tile(c_q_g[o][:half], reps), jnp.tile(c_q_g[o][half:], reps)])
                  * (HDIM ** -0.5 * LOG2E))
            gk = jnp.stack([jnp.tile(c_k_g[o][:half], reps), jnp.tile(c_k_g[o][half:], reps)])
            seg = np.arange(LANES) // half
            bd = jnp.asarray(seg[:, None] == seg[None, :], BF16)
            w_out = w_out_c[o].astype(BF16).reshape(n_kv, per, HDIM, d).transpose(1, 0, 2, 3).reshape(nq, d)
            q, k, v = _inproj_c(h, lat[0], lat[1], g_pre_m, wq, wk, wv, gq, gk, bd, cos_l, sin_l, True)
            kc, vc = _inproj_c(hc, cx[0], cx[1], g_pre_m, wq, wk, wv, gq, gk, bd, cos_c, sin_c, False)
            y = _gqa_attention(q, k, v, per_batch(kc), per_batch(vc))
            h = _outproj([y], [w_out], h, lat[2], g_post_m)
            if with_ctx:
                raise NotImplementedError("context update after a grouped-query layer")
        wts = _ffn_weights(w_up[layer], conv_w[layer], conv_b[layer], w_down[layer])
        h = _ffn(h, lat[3], lat[4], lat[5], g_pre_f, g_post_f, wts)
        if with_ctx:
            hc = _ffn(hc, cx[3], cx[4], cx[5], g_pre_f, g_post_f, wts, seq_len=lc)
    return h
```

```python
import functools

import numpy as np
import jax
import jax.numpy as jnp
from jax import lax
from jax.experimental import pallas as pl
from jax.experimental.pallas import tpu as pltpu

F32 = jnp.float32
BF16 = jnp.bfloat16

EPS = 1e-6
NEG_INF = -1e30
GRID_W = 64
ROPE_THETA = 10000.0
LOG2E = 1.4426950408889634
A_CHUNK = 128
A_GROUPS = 4
NA_ROWS = 8
NA_COLS = 16
HDIM = 64
LANES = 128
NA_QROWS = 4
NA_KROWS = 12
SUBLANES = 8
FFN_CW = 256
V7X_VMEM_BYTES = 64 * 1024 * 1024


def _cparams(sem, vmem_mb):
    assert vmem_mb * 1024 * 1024 < V7X_VMEM_BYTES
    return pltpu.CompilerParams(dimension_semantics=sem, vmem_limit_bytes=vmem_mb * 1024 * 1024)


def _tile(n, pref):
    return pref if n % pref == 0 else n


def _vec_spec(vec):
    table, row = vec
    return pl.BlockSpec((1, 1, table.shape[-1]), lambda bi, *_: (row(bi), 0, 0))


def _rms(x, g):
    return x * lax.rsqrt(jnp.mean(x * x, axis=-1, keepdims=True) + EPS) * g


def _modulate(x, g, shift, scale):
    return _rms(x, g) * (1.0 + scale) + shift


def _gelu(x):
    return 0.5 * x * (1.0 + jnp.tanh(0.7978845608028654 * (x + 0.044715 * (x * x * x))))


def _silu(x):
    return x * jax.nn.sigmoid(x)


def _dot(a, b):
    return jnp.dot(a, b, preferred_element_type=F32)


def _dot_t(a, b):
    return lax.dot_general(a, b, (((1,), (1,)), ((), ())), preferred_element_type=F32)


def _ada_body(c_ref, w_ref, b_ref, o_ref):
    s = _silu(c_ref[...])
    o_ref[0] = jnp.dot(s, w_ref[0], preferred_element_type=F32,
                       precision=lax.Precision.HIGHEST) + b_ref[0]


def _ada(cv, w_ada, b_ada):
    depth, d, n = w_ada.shape
    rows = cv.shape[0]
    tn = _tile(n, 1536)
    return pl.pallas_call(
        _ada_body,
        grid=(depth, n // tn),
        in_specs=[pl.BlockSpec((rows, d), lambda l, j: (0, 0)),
                  pl.BlockSpec((1, d, tn), lambda l, j: (l, 0, j)),
                  pl.BlockSpec((1, 1, tn), lambda l, j: (l, 0, j))],
        out_specs=pl.BlockSpec((1, rows, tn), lambda l, j: (l, 0, j)),
        out_shape=jax.ShapeDtypeStruct((depth, rows, n), F32),
        compiler_params=_cparams(("parallel", "parallel"), 40),
        name="ada",
    )(cv, w_ada, b_ada.reshape(depth, 1, n))


def _store_ones_padded(v_ref, rows, vals):
    lo = lax.broadcasted_iota(jnp.int32, (vals.shape[0], LANES), 1) < HDIM
    for t in range(vals.shape[1] // LANES):
        vt = vals[:, t * LANES:(t + 1) * LANES]
        v_ref[0, rows, (2 * t) * LANES:(2 * t + 1) * LANES] = jnp.where(lo, vt, 1.0).astype(BF16)
        v_ref[0, rows, (2 * t + 1) * LANES:(2 * t + 2) * LANES] = jnp.where(lo, 1.0, vt).astype(BF16)


def _inproj_ab_body(h_ref, sh_ref, sc_ref, g_ref, w_ref, ws_ref, bs_ref, gv_ref,
                    ya_ref, q_ref, k_ref, v_ref, *, sub, n_sub, aw):
    bw = q_ref.shape[-1]
    gd = aw // A_GROUPS
    bounds = [0, aw, 2 * aw, 2 * aw + bw, 2 * aw + 2 * bw, 2 * aw + 3 * bw]
    projs = []
    for t in range(n_sub):
        xm = _modulate(h_ref[0, t * sub:(t + 1) * sub], g_ref[0], sh_ref[0], sc_ref[0]).astype(BF16)
        projs.append([_dot(xm, w_ref[:, c0:c1]) for c0, c1 in zip(bounds[:-1], bounds[1:])])
    for t in range(n_sub):
        rows = slice(t * sub, (t + 1) * sub)
        tu, tva, tq, tk, tv = projs[t]
        q_ref[0, rows] = (tq * (HDIM ** -0.5 * LOG2E)).astype(BF16)
        k_ref[0, rows] = tk.astype(BF16)
        _store_ones_padded(v_ref, rows, tv)
        u = _gelu(tu)
        va = _gelu(tva)
        mu = jnp.mean(va, axis=-1, keepdims=True)
        d = va - mu
        var = jnp.mean(d * d, axis=-1, keepdims=True)
        vn = (d * lax.rsqrt(var + EPS) * gv_ref[...]).astype(BF16)
        for n in range(sub // A_CHUNK):
            r0, r1 = n * A_CHUNK, (n + 1) * A_CHUNK
            for g in range(A_GROUPS):
                c0, c1 = g * gd, (g + 1) * gd
                s = _dot(ws_ref[g], vn[r0:r1, c0:c1]) + bs_ref[g]
                ya_ref[0, t * sub + r0:t * sub + r1, c0:c1] = (u[r0:r1, c0:c1] * s).astype(BF16)


def _inproj_ab(h, sh, sc, g_pre, w_in, w_s, b_s_b, g_v, aw, bw):
    b, l, d = h.shape
    sub = _tile(l, 512)
    n_sub = 2 if l % (2 * sub) == 0 else 1
    tm = sub * n_sub
    n_in = w_in.shape[1]
    row = lambda bi, i: (bi, i, 0)
    c2 = lambda bi, i: (0, 0)
    c3 = lambda bi, i: (0, 0, 0)
    body = functools.partial(_inproj_ab_body, sub=sub, n_sub=n_sub, aw=aw)
    widths = (aw, bw, bw, 2 * bw)
    return pl.pallas_call(
        body,
        grid=(b, l // tm),
        in_specs=[pl.BlockSpec((1, tm, d), row), _vec_spec(sh), _vec_spec(sc), _vec_spec(g_pre),
                  pl.BlockSpec((d, n_in), c2),
                  pl.BlockSpec(w_s.shape, c3), pl.BlockSpec(b_s_b.shape, c3),
                  pl.BlockSpec((1, aw), c2)],
        out_specs=[pl.BlockSpec((1, tm, w), row) for w in widths],
        out_shape=[jax.ShapeDtypeStruct((b, l, w), BF16) for w in widths],
        compiler_params=_cparams(("parallel", "parallel"), 48),
        name="inproj_ab",
    )(h, sh[0], sc[0], g_pre[0], w_in, w_s, b_s_b, g_v)


def _softmax_pv(scores, values):
    m = scores[0].max(axis=-1, keepdims=True)
    for s in scores[1:]:
        m = jnp.maximum(m, s.max(axis=-1, keepdims=True))
    o = _dot(jnp.exp2(scores[0] - m).astype(BF16), values[0])
    for s, v in zip(scores[1:], values[1:]):
        o = o + _dot(jnp.exp2(s - m).astype(BF16), v)
    return o


def _normalised_pair(o_even, o_odd):
    lo = lax.broadcasted_iota(jnp.int32, o_even.shape, 1) < HDIM
    return jnp.where(lo, o_even / pltpu.roll(o_even, HDIM, axis=1), o_odd / pltpu.roll(o_odd, HDIM, axis=1))


def _na_body(q_ref, k_ref, v_ref, kc_ref, vc_ref, t_ref, o_ref, *, rows, n_blocks):
    for u in range(n_blocks):
        _na_block(q_ref, k_ref, v_ref, kc_ref, vc_ref, t_ref, o_ref,
                  n_blocks * pl.program_id(1) + u, u * NA_QROWS * GRID_W, rows)


def _na_block(q_ref, k_ref, v_ref, kc_ref, vc_ref, t_ref, o_ref, rb, qoff, rows):
    tq = NA_QROWS * GRID_W
    nk = NA_KROWS * GRID_W
    start = jnp.clip(NA_QROWS * rb - NA_ROWS // 2, 0, rows - NA_KROWS)
    koff = pl.multiple_of(start * GRID_W, NA_QROWS * GRID_W)
    lo_q = lax.broadcasted_iota(jnp.int32, (tq, LANES), 1) < HDIM
    lo_b = lax.broadcasted_iota(jnp.int32, (GRID_W, LANES), 1) < GRID_W
    idx = []
    for ri in range(NA_QROWS):
        r = NA_QROWS * rb + ri
        r0 = jnp.clip(r - NA_ROWS // 2, 0, rows - NA_ROWS)
        row = []
        for j in range(NA_KROWS):
            kr = start + j
            valid = jnp.logical_and(kr >= r0, kr < r0 + NA_ROWS)
            row.append(jnp.where(valid, kr - r + NA_ROWS - 1, 2 * NA_ROWS - 1))
        idx.append(row)
    for p in range(q_ref.shape[-1] // LANES):
        cp = slice(p * LANES, (p + 1) * LANES)
        q = q_ref[0, qoff:qoff + tq, cp]
        zero = jnp.zeros_like(q)
        lhs = jnp.concatenate([jnp.where(lo_q, q, zero), jnp.where(lo_q, zero, q)], axis=0)
        bias = jnp.concatenate([
            jnp.concatenate([
                jnp.where(lo_b, t_ref[2 * p + a, idx[ri][2 * m]], t_ref[2 * p + a, idx[ri][2 * m + 1]])
                for m in range(NA_KROWS // 2)], axis=1)
            for a in range(2) for ri in range(NA_QROWS)], axis=0)
        s_loc = _dot_t(lhs, k_ref[0, pl.ds(koff, nk), cp]) + bias
        s_ctx = _dot_t(lhs, kc_ref[0, :, cp])
        m = jnp.maximum(s_loc.max(axis=-1, keepdims=True), s_ctx.max(axis=-1, keepdims=True))
        p_loc = jnp.exp2(s_loc - m).astype(BF16)
        p_ctx = jnp.exp2(s_ctx - m).astype(BF16)
        outs = []
        for a in range(2):
            ch = slice((2 * p + a) * LANES, (2 * p + a + 1) * LANES)
            rs = slice(a * tq, (a + 1) * tq)
            outs.append(_dot(p_loc[rs], v_ref[0, pl.ds(koff, nk), ch]) + _dot(p_ctx[rs], vc_ref[0, :, ch]))
        o_ref[0, qoff:qoff + tq, cp] = _normalised_pair(outs[0], outs[1]).astype(BF16)


def _na_attention(q, k, v, kc, vc, tdup):
    b, s, w = q.shape
    lc = kc.shape[1]
    rows = s // GRID_W
    n_blocks = 4
    assert rows % (n_blocks * NA_QROWS) == 0
    tq = n_blocks * NA_QROWS * GRID_W
    body = functools.partial(_na_body, rows=rows, n_blocks=n_blocks)
    whole = lambda n, wd: pl.BlockSpec((1, n, wd), lambda bi, r: (bi, 0, 0))
    return pl.pallas_call(
        body,
        grid=(b, rows // (n_blocks * NA_QROWS)),
        in_specs=[pl.BlockSpec((1, tq, w), lambda bi, r: (bi, r, 0)),
                  whole(s, w), whole(s, 2 * w), whole(lc, w), whole(lc, 2 * w),
                  pl.BlockSpec(tdup.shape, lambda bi, r: (0, 0, 0, 0))],
        out_specs=pl.BlockSpec((1, tq, w), lambda bi, r: (bi, r, 0)),
        out_shape=jax.ShapeDtypeStruct((b, s, w), BF16),
        compiler_params=_cparams(("parallel", "arbitrary"), 48),
        name="na_attention",
    )(q, k, v, kc, vc, tdup)


def _ctx_attn_body(q_ref, k_ref, v_ref, o_ref):
    lo = lax.broadcasted_iota(jnp.int32, (q_ref.shape[1], LANES), 1) < HDIM
    for p in range(q_ref.shape[-1] // LANES):
        cp = slice(p * LANES, (p + 1) * LANES)
        q = q_ref[0, :, cp]
        k = k_ref[0, :, cp]
        outs = []
        for a in range(2):
            ch = slice((2 * p + a) * LANES, (2 * p + a + 1) * LANES)
            qa = jnp.where(lo if a == 0 else jnp.logical_not(lo), q, jnp.zeros_like(q))
            outs.append(_softmax_pv([_dot_t(qa, k)], [v_ref[0, :, ch]]))
        o_ref[0, :, cp] = _normalised_pair(outs[0], outs[1]).astype(BF16)


def _ctx_attention(q, k, v):
    b, l, w = q.shape
    spec = pl.BlockSpec((1, l, w), lambda bi: (bi, 0, 0))
    return pl.pallas_call(
        _ctx_attn_body,
        grid=(b,),
        in_specs=[spec, spec, pl.BlockSpec((1, l, 2 * w), lambda bi: (bi, 0, 0))],
        out_specs=spec,
        out_shape=jax.ShapeDtypeStruct((b, l, w), BF16),
        compiler_params=_cparams(("parallel",), 32),
        name="ctx_attention",
    )(q, k, v)


def _outproj_body(*refs, n_parts):
    y_refs = refs[:n_parts]
    w_refs = refs[n_parts:2 * n_parts]
    h_ref, gt_ref, g_ref, o_ref = refs[2 * n_parts:]
    y = _dot(y_refs[0][0], w_refs[0][...])
    for yr, wr in zip(y_refs[1:], w_refs[1:]):
        y = y + _dot(yr[0], wr[...])
    o_ref[0] = h_ref[0] + gt_ref[0] * _rms(y, g_ref[0])


def _outproj(ys, ws, h, gt, g_post):
    b, l, d = h.shape
    tm = _tile(l, 1024)
    row = lambda bi, i: (bi, i, 0)
    c2 = lambda bi, i: (0, 0)
    body = functools.partial(_outproj_body, n_parts=len(ys))
    return pl.pallas_call(
        body,
        grid=(b, l // tm),
        in_specs=([pl.BlockSpec((1, tm, y.shape[-1]), row) for y in ys]
                  + [pl.BlockSpec(w.shape, c2) for w in ws]
                  + [pl.BlockSpec((1, tm, d), row), _vec_spec(gt), _vec_spec(g_post)]),
        out_specs=pl.BlockSpec((1, tm, d), row),
        out_shape=jax.ShapeDtypeStruct((b, l, d), F32),
        compiler_params=_cparams(("parallel", "parallel"), 40),
        name="outproj",
    )(*ys, *ws, h, gt[0], g_post[0])


def _ffn_body(xp_ref, x_ref, xn_ref, sh_ref, sc_ref, gt_ref, gpre_ref, gpost_ref,
              wu_ref, cw_ref, cb_ref, wd_ref,
              o_ref, xs_ref, p3_ref, act_ref, acc_ref, *, tm, n_tiles, n_chunks, groups, seq):
    i = pl.program_id(1)
    g = gpre_ref[0]
    sh = sh_ref[0]
    sc = sc_ref[0]
    nv = tm // SUBLANES
    nl = x_ref.shape[-1] // LANES
    xm = _modulate(x_ref[0], g, sh, sc)
    for j in range(nl):
        for s in range(SUBLANES):
            p3_ref[j, pl.ds(s, nv, stride=SUBLANES), :] = xm[s * nv:(s + 1) * nv, j * LANES:(j + 1) * LANES]
    xs_ref[0:tm] = jnp.concatenate([p3_ref[j] for j in range(nl)], axis=1).astype(BF16)
    rid = lax.broadcasted_iota(jnp.int32, (2 * SUBLANES, 1), 0)
    keep = jnp.where(rid == 0, (i > 0).astype(F32), jnp.where(rid == 1, (i < n_tiles - 1).astype(F32), 0.0))
    halo = jnp.concatenate([xp_ref[0, SUBLANES - 1:SUBLANES], xn_ref[0, 0:1],
                            jnp.zeros((2 * SUBLANES - 2, x_ref.shape[-1]), F32)], axis=0)
    xs_ref[tm:tm + 2 * SUBLANES] = (_modulate(halo, g, sh, sc) * keep).astype(BF16)

    srow = lax.broadcasted_iota(jnp.int32, (SUBLANES, FFN_CW), 0)
    inner_start = functools.reduce(jnp.logical_or, [srow == s for s in range(1, SUBLANES) if (s * nv) % seq == 0],
                                   srow < 0)
    inner_end = functools.reduce(jnp.logical_or,
                                 [srow == s for s in range(SUBLANES - 1) if ((s + 1) * nv) % seq == 0], srow < 0)

    def conv(hh, cw, cb):
        hm = hh[0:tm]
        first = jnp.where(srow == 0, hh[tm:tm + 1], pltpu.roll(hm[tm - SUBLANES:tm], 1, axis=0))
        last = jnp.where(srow == SUBLANES - 1, hh[tm + 1:tm + 2], pltpu.roll(hm[0:SUBLANES], SUBLANES - 1, axis=0))
        first = jnp.where(inner_start, 0.0, first)
        last = jnp.where(inner_end, 0.0, last)
        prev = jnp.concatenate([first, hm[0:tm - SUBLANES]], axis=0)
        nxt = jnp.concatenate([hm[SUBLANES:tm], last], axis=0)
        return prev * cw[0:1] + hm * cw[1:2] + nxt * cw[2:3] + cb

    ff = n_chunks * FFN_CW

    def up(c):
        xs = xs_ref[...]
        return (_dot(xs, wu_ref[:, c * FFN_CW:(c + 1) * FFN_CW]),
                _dot(xs, wu_ref[:, ff + c * FFN_CW:ff + (c + 1) * FFN_CW]))

    pending = up(0)
    k0 = 0
    for c in range(n_chunks):
        ha, hg = pending
        if c + 1 < n_chunks:
            pending = up(c + 1)
        cc = slice(c * FFN_CW, (c + 1) * FFN_CW)
        cg = slice(ff + c * FFN_CW, ff + (c + 1) * FFN_CW)
        a = conv(ha, cw_ref[:, cc], cb_ref[:, cc])
        gg = conv(hg, cw_ref[:, cg], cb_ref[:, cg])
        act_ref[:, cc] = (_silu(gg) * a).astype(BF16)
        if c + 1 in groups:
            k1 = (c + 1) * FFN_CW
            part = _dot(act_ref[:, k0:k1], wd_ref[k0:k1, :])
            if k0 == 0:
                acc_ref[...] = part
            else:
                acc_ref[...] += part
            k0 = k1
    f = gt_ref[0] * _rms(acc_ref[...], gpost_ref[0])
    for j in range(nl):
        p3_ref[j] = f[:, j * LANES:(j + 1) * LANES]
    for s in range(SUBLANES):
        fs = jnp.concatenate([p3_ref[j, pl.ds(s, nv, stride=SUBLANES), :] for j in range(nl)], axis=1)
        o_ref[0, s * nv:(s + 1) * nv, :] = x_ref[0, s * nv:(s + 1) * nv, :] + fs


def _ffn(h, sh, sc, gt, g_pre, g_post, wts, seq_len=None):
    wu, cw, cb, wd = wts
    b, l, d = h.shape
    tm = _tile(l, 1024)
    n_tiles = l // tm
    seq = l if seq_len is None else seq_len
    assert seq == l or (l == tm and l % seq == 0 and seq % (tm // SUBLANES) == 0)
    ff = wd.shape[0]
    n_chunks = ff // FFN_CW
    groups = (6, n_chunks)
    hb = tm // SUBLANES
    last_hb = l // SUBLANES - 1
    row = lambda bi, i: (bi, i, 0)
    c2 = lambda bi, i: (0, 0)
    whole = lambda a: pl.BlockSpec(a.shape, lambda bi, i: (0,) * a.ndim, pipeline_mode=pl.Buffered(1))
    body = functools.partial(_ffn_body, tm=tm, n_tiles=n_tiles, n_chunks=n_chunks, groups=groups, seq=seq)
    return pl.pallas_call(
        body,
        grid=(b, n_tiles),
        in_specs=[pl.BlockSpec((1, SUBLANES, d), lambda bi, i: (bi, jnp.maximum(i * hb - 1, 0), 0)),
                  pl.BlockSpec((1, tm, d), row),
                  pl.BlockSpec((1, SUBLANES, d), lambda bi, i: (bi, jnp.minimum((i + 1) * hb, last_hb), 0)),
                  _vec_spec(sh), _vec_spec(sc), _vec_spec(gt), _vec_spec(g_pre), _vec_spec(g_post),
                  whole(wu), whole(cw), whole(cb), whole(wd)],
        out_specs=pl.BlockSpec((1, tm, d), row),
        out_shape=jax.ShapeDtypeStruct((b, l, d), F32),
        scratch_shapes=[pltpu.VMEM((tm + 2 * SUBLANES, d), BF16), pltpu.VMEM((d // LANES, tm, LANES), F32),
                        pltpu.VMEM((tm, ff), BF16), pltpu.VMEM((tm, d), F32)],
        compiler_params=_cparams(("parallel", "parallel"), 56),
        name="conv_ffn",
    )(h, h, h, sh[0], sc[0], gt[0], g_pre[0], g_post[0], wu, cw, cb, wd)


def _ffn_weights(w_up, conv_w, conv_b, w_down):
    return w_up.astype(BF16), conv_w, conv_b[None, :], w_down.astype(BF16)


def _inproj_c_body(h_ref, sh_ref, sc_ref, g_ref, wq_ref, wk_ref, wv_ref, gq_ref, gk_ref, bd_ref,
                   cos_ref, sin_ref, *out_refs, n_qtiles, sub, n_sub):
    bd = bd_ref[...]
    if n_qtiles:
        q_ref, k_ref, v_ref = out_refs
    else:
        k_ref, v_ref = out_refs
    projs = []
    for t in range(n_sub):
        xm = _modulate(h_ref[0, t * sub:(t + 1) * sub], g_ref[0], sh_ref[0], sc_ref[0]).astype(BF16)
        tq = _dot(xm, wq_ref[...]) if n_qtiles else None
        projs.append((tq, _dot(xm, wk_ref[...]), _dot(xm, wv_ref[...])))
    for t in range(n_sub):
        rows = slice(t * sub, (t + 1) * sub)
        tq, tk, tv = projs[t]
        _store_ones_padded(v_ref, rows, tv)
        tiles = [(tq[:, 2 * LANES * j:2 * LANES * j + LANES], tq[:, 2 * LANES * j + LANES:2 * LANES * (j + 1)])
                 for j in range(n_qtiles)] + [(tk[:, :LANES], tk[:, LANES:])]
        sq = jnp.concatenate([(t1 * t1 + t2 * t2).astype(BF16) for t1, t2 in tiles], axis=0)
        ss = _dot(sq, bd)
        cs = cos_ref[rows]
        sn = sin_ref[rows]
        for j, (t1, t2) in enumerate(tiles):
            is_q = j < n_qtiles
            o_ref = q_ref if is_q else k_ref
            gains = gq_ref[...] if is_q else gk_ref[...]
            c0 = 2 * LANES * j if is_q else 0
            r = lax.rsqrt(ss[j * sub:(j + 1) * sub] * (1.0 / HDIM) + EPS)
            a1 = t1 * r * gains[0:1]
            a2 = t2 * r * gains[1:2]
            o_ref[0, rows, c0:c0 + LANES] = (a1 * cs - a2 * sn).astype(BF16)
            o_ref[0, rows, c0 + LANES:c0 + 2 * LANES] = (a1 * sn + a2 * cs).astype(BF16)


def _inproj_c(h, sh, sc, g_pre, wq, wk, wv, gq, gk, bd, cos, sin, with_q):
    b, l, d = h.shape
    sub = _tile(l, 512)
    n_sub = 2 if l % (2 * sub) == 0 else 1
    tm = sub * n_sub
    row = lambda bi, i: (bi, i, 0)
    c2 = lambda bi, i: (0, 0)
    nq, nkv = wq.shape[1], wk.shape[1]
    n_qtiles = nq // (2 * LANES) if with_q else 0
    body = functools.partial(_inproj_c_body, n_qtiles=n_qtiles, sub=sub, n_sub=n_sub)
    out_specs = [pl.BlockSpec((1, tm, nkv), row), pl.BlockSpec((1, tm, 2 * nkv), row)]
    out_shape = [jax.ShapeDtypeStruct((b, l, nkv), BF16), jax.ShapeDtypeStruct((b, l, 2 * nkv), BF16)]
    if with_q:
        out_specs = [pl.BlockSpec((1, tm, nq), row)] + out_specs
        out_shape = [jax.ShapeDtypeStruct((b, l, nq), BF16)] + out_shape
    return pl.pallas_call(
        body,
        grid=(b, l // tm),
        in_specs=[pl.BlockSpec((1, tm, d), row), _vec_spec(sh), _vec_spec(sc), _vec_spec(g_pre),
                  pl.BlockSpec(wq.shape, c2), pl.BlockSpec(wk.shape, c2), pl.BlockSpec(wv.shape, c2),
                  pl.BlockSpec(gq.shape, c2), pl.BlockSpec(gk.shape, c2), pl.BlockSpec(bd.shape, c2),
                  pl.BlockSpec((tm, LANES), lambda bi, i: (i, 0)),
                  pl.BlockSpec((tm, LANES), lambda bi, i: (i, 0))],
        out_specs=out_specs,
        out_shape=out_shape,
        compiler_params=_cparams(("parallel", "parallel"), 40),
        name="inproj_c",
    )(h, sh[0], sc[0], g_pre[0], wq, wk, wv, gq, gk, bd, cos, sin)


def _gqa_body(q_ref, k_ref, v_ref, kc_ref, vc_ref, o_ref, *, tq, n_sub):
    k = k_ref[0]
    kc = kc_ref[0]
    seg = (lax.broadcasted_iota(jnp.int32, (tq, 2 * LANES), 1) % LANES) // (HDIM // 2)

    def scores(t):
        q = q_ref[0, t * tq:(t + 1) * tq]
        zero = jnp.zeros_like(q)
        lhs = jnp.concatenate([jnp.where(seg == s, q, zero) for s in range(4)], axis=0)
        return _dot_t(lhs, k), _dot_t(lhs, kc)

    pending = scores(0)
    for t in range(n_sub):
        s_lat, s_ctx = pending
        if t + 1 < n_sub:
            pending = scores(t + 1)
        m = jnp.maximum(s_lat.max(axis=-1, keepdims=True), s_ctx.max(axis=-1, keepdims=True))
        p_lat = jnp.exp2(s_lat - m).astype(BF16)
        p_ctx = jnp.exp2(s_ctx - m).astype(BF16)
        outs = []
        for s in range(4):
            rs = slice(s * tq, (s + 1) * tq)
            cs = slice(s * LANES, (s + 1) * LANES)
            outs.append(_dot(p_lat[rs], v_ref[0, :, cs]) + _dot(p_ctx[rs], vc_ref[0, :, cs]))
        o_ref[0, t * tq:(t + 1) * tq] = jnp.concatenate(
            [_normalised_pair(outs[0], outs[1]), _normalised_pair(outs[2], outs[3])], axis=1).astype(BF16)


def _gqa_attention(q, k, v, kc, vc):
    b, s, nq = q.shape
    lc = kc.shape[1]
    nkv = k.shape[-1]
    n_sub = 4
    tq = 256 * n_sub
    body = functools.partial(_gqa_body, tq=tq // n_sub, n_sub=n_sub)
    kv_spec = lambda n, w: pl.BlockSpec((1, n, w), lambda bi, i, j: (bi, 0, 0))
    return pl.pallas_call(
        body,
        grid=(b, s // tq, nq // (2 * LANES)),
        in_specs=[pl.BlockSpec((1, tq, 2 * LANES), lambda bi, i, j: (bi, i, j)),
                  kv_spec(s, nkv), kv_spec(s, 2 * nkv), kv_spec(lc, nkv), kv_spec(lc, 2 * nkv)],
        out_specs=pl.BlockSpec((1, tq, 2 * LANES), lambda bi, i, j: (bi, i, j)),
        out_shape=jax.ShapeDtypeStruct((b, s, nq), BF16),
        compiler_params=_cparams(("parallel", "arbitrary", "arbitrary"), 56),
        name="gqa_attention",
    )(q, k, v, kc, vc)


def _na_bias_table(rpb):
    qc = np.arange(GRID_W)[:, None]
    kc = np.arange(GRID_W)[None, :]
    c0 = np.clip(qc - NA_COLS // 2, 0, GRID_W - NA_COLS)
    in_win = (kc >= c0) & (kc < c0 + NA_COLS)
    n_h, n_dr, n_dc = rpb.shape
    period = 2 * GRID_W
    lead = GRID_W - NA_COLS
    sig = jnp.pad(rpb, ((0, 0), (0, 0), (lead, period - n_dc - lead)))
    toep = jnp.tile(sig, (1, 1, GRID_W))[..., :GRID_W * (period - 1)].reshape(n_h, n_dr, GRID_W, period - 1)
    toep = toep[..., GRID_W - 1:]
    t = jnp.where(in_win, toep * LOG2E, NEG_INF)
    t = jnp.concatenate([t, jnp.full_like(t[:, :1], NEG_INF)], axis=1)
    return jnp.concatenate([t, t], axis=-1)


def _rope_tables(n_tokens):
    t = np.arange(n_tokens)
    quarter = HDIM // 4
    inv = (ROPE_THETA ** (-np.arange(quarter, dtype=np.float32) / quarter)).astype(np.float32)
    rows = (t // GRID_W).astype(np.float32)[:, None] * inv
    cols = (t % GRID_W).astype(np.float32)[:, None] * inv
    ang = np.tile(np.concatenate([rows, cols], axis=-1), (1, LANES // (HDIM // 2)))
    return jnp.asarray(np.cos(ang), F32), jnp.asarray(np.sin(ang), F32)


def kernel(x, c, ctx, c_ctx, w_ada, b_ada, norm_g, w_in_ab, a_w_s, a_b_s, a_v_g, b_rpb, w_out_ab,
           w_qkv_c, c_q_g, c_k_g, w_out_c, w_up, conv_w, conv_b, w_down):
    bsz, seq, d = x.shape
    lc = ctx.shape[1]
    depth = w_ada.shape[0]
    aw = a_v_g.shape[-1]
    n_heads_b = b_rpb.shape[1]
    bw = n_heads_b * HDIM
    n_kv = 4
    n_heads_c = w_out_c.shape[1] // HDIM
    assert seq % (NA_QROWS * GRID_W) == 0 and seq // GRID_W >= NA_KROWS
    assert aw // A_GROUPS == A_CHUNK and w_up.shape[-1] // 2 % FFN_CW == 0

    n_rows = 8 * ((bsz + 1 + 7) // 8)
    cv = jnp.concatenate([c, c_ctx[None], jnp.zeros((n_rows - bsz - 1, d), F32)], axis=0)
    mods = _ada(cv, w_ada, b_ada).reshape(depth * n_rows * 6, 1, d)
    gains = norm_g.reshape(depth * 4, 1, d)

    def mod_vec(layer, j, is_ctx):
        base = layer * n_rows * 6 + j
        return mods, ((lambda bi: base + bsz * 6) if is_ctx else (lambda bi: base + bi * 6))

    def gain_vec(layer, j):
        return gains, (lambda bi: layer * 4 + j)

    merge = 2 if bsz % 2 == 0 and lc % A_CHUNK == 0 else 1
    merged = lambda a: a.reshape(bsz // merge, merge * lc, a.shape[-1])
    per_batch = lambda a: a.reshape(bsz, lc, a.shape[-1])

    cos_l, sin_l = _rope_tables(seq)
    cos_c, sin_c = jnp.ones((merge * lc, LANES), F32), jnp.zeros((merge * lc, LANES), F32)

    h, hc = x, merged(ctx)
    for layer in range(depth):
        with_ctx = layer < depth - 1
        lat = [mod_vec(layer, j, False) for j in range(6)]
        cx = [mod_vec(layer, j, True) for j in range(6)]
        g_pre_m, g_post_m, g_pre_f, g_post_f = [gain_vec(layer, j) for j in range(4)]
        if layer % 2 == 0:
            e = layer // 2
            w_in = w_in_ab[e].astype(BF16)
            w_s = a_w_s[e].astype(BF16)
            b_s_b = jnp.broadcast_to(a_b_s[e][:, :, None], a_w_s[e].shape[:2] + (aw // A_GROUPS,))
            g_v = a_v_g[e][None, :]
            w_out = w_out_ab[e].astype(BF16)
            ya, q, k, v = _inproj_ab(h, lat[0], lat[1], g_pre_m, w_in, w_s, b_s_b, g_v, aw, bw)
            yca, qc, kc, vc = _inproj_ab(hc, cx[0], cx[1], g_pre_m, w_in, w_s, b_s_b, g_v, aw, bw)
            qc, kc, vc = per_batch(qc), per_batch(kc), per_batch(vc)
            yb = _na_attention(q, k, v, kc, vc, _na_bias_table(b_rpb[e]))
            h = _outproj([ya, yb], [w_out[:aw], w_out[aw:]], h, lat[2], g_post_m)
            if with_ctx:
                ycb = merged(_ctx_attention(qc, kc, vc))
                hc = _outproj([yca, ycb], [w_out[:aw], w_out[aw:]], hc, cx[2], g_post_m)
        else:
            o = layer // 2
            nq = n_heads_c * HDIM
            nkv = n_kv * HDIM
            per = n_heads_c // n_kv
            half = HDIM // 2
            wqkv = w_qkv_c[o].astype(BF16)
            wq = wqkv[:, :nq].reshape(d, n_kv, per, 2, half).transpose(0, 2, 3, 1, 4).reshape(d, nq)
            wk = wqkv[:, nq:nq + nkv].reshape(d, n_kv, 2, half).transpose(0, 2, 1, 3).reshape(d, nkv)
            wv = wqkv[:, nq + nkv:]
            reps = LANES // half
            gq = (jnp.stack([jnp.tile(c_q_g[o][:half], reps), jnp.tile(c_q_g[o][half:], reps)])
                  * (HDIM ** -0.5 * LOG2E))
            gk = jnp.stack([jnp.tile(c_k_g[o][:half], reps), jnp.tile(c_k_g[o][half:], reps)])
            seg = np.arange(LANES) // half
            bd = jnp.asarray(seg[:, None] == seg[None, :], BF16)
            w_out = w_out_c[o].astype(BF16).reshape(n_kv, per, HDIM, d).transpose(1, 0, 2, 3).reshape(nq, d)
            q, k, v = _inproj_c(h, lat[0], lat[1], g_pre_m, wq, wk, wv, gq, gk, bd, cos_l, sin_l, True)
            kc, vc = _inproj_c(hc, cx[0], cx[1], g_pre_m, wq, wk, wv, gq, gk, bd, cos_c, sin_c, False)
            y = _gqa_attention(q, k, v, per_batch(kc), per_batch(vc))
            h = _outproj([y], [w_out], h, lat[2], g_post_m)
            if with_ctx:
                raise NotImplementedError("context update after a grouped-query layer")
        wts = _ffn_weights(w_up[layer], conv_w[layer], conv_b[layer], w_down[layer])
        h = _ffn(h, lat[3], lat[4], lat[5], g_pre_f, g_post_f, wts)
        if with_ctx:
            hc = _ffn(hc, cx[3], cx[4], cx[5], g_pre_f, g_post_f, wts, seq_len=lc)
    return h
```

```python
import functools

import numpy as np
import jax
import jax.numpy as jnp
from jax import lax
from jax.experimental import pallas as pl
from jax.experimental.pallas import tpu as pltpu

F32 = jnp.float32
BF16 = jnp.bfloat16

EPS = 1e-6
NEG_INF = -1e30
GRID_W = 64
ROPE_THETA = 10000.0
LOG2E = 1.4426950408889634
A_CHUNK = 128
A_GROUPS = 4
NA_ROWS = 8
NA_COLS = 16
HDIM = 64
LANES = 128
NA_QROWS = 4
NA_KROWS = 12
SUBLANES = 8
FFN_CW = 256
V7X_VMEM_BYTES = 64 * 1024 * 1024


def _cparams(sem, vmem_mb):
    assert vmem_mb * 1024 * 1024 < V7X_VMEM_BYTES
    return pltpu.CompilerParams(dimension_semantics=sem, vmem_limit_bytes=vmem_mb * 1024 * 1024)


def _tile(n, pref):
    return pref if n % pref == 0 else n


def _vec_spec(vec):
    table, row = vec
    return pl.BlockSpec((1, 1, table.shape[-1]), lambda bi, *_: (row(bi), 0, 0))


def _rms(x, g):
    return x * lax.rsqrt(jnp.mean(x * x, axis=-1, keepdims=True) + EPS) * g


def _modulate(x, g, shift, scale):
    return _rms(x, g) * (1.0 + scale) + shift


def _gelu(x):
    return 0.5 * x * (1.0 + jnp.tanh(0.7978845608028654 * (x + 0.044715 * (x * x * x))))


def _silu(x):
    return x * jax.nn.sigmoid(x)


def _dot(a, b):
    return jnp.dot(a, b, preferred_element_type=F32)


def _dot_t(a, b):
    return lax.dot_general(a, b, (((1,), (1,)), ((), ())), preferred_element_type=F32)


def _ada_body(c_ref, w_ref, b_ref, o_ref):
    s = _silu(c_ref[...])
    o_ref[0] = jnp.dot(s, w_ref[0], preferred_element_type=F32,
                       precision=lax.Precision.HIGHEST) + b_ref[0]


def _ada(cv, w_ada, b_ada):
    depth, d, n = w_ada.shape
    rows = cv.shape[0]
    tn = _tile(n, 1536)
    return pl.pallas_call(
        _ada_body,
        grid=(depth, n // tn),
        in_specs=[pl.BlockSpec((rows, d), lambda l, j: (0, 0)),
                  pl.BlockSpec((1, d, tn), lambda l, j: (l, 0, j)),
                  pl.BlockSpec((1, 1, tn), lambda l, j: (l, 0, j))],
        out_specs=pl.BlockSpec((1, rows, tn), lambda l, j: (l, 0, j)),
        out_shape=jax.ShapeDtypeStruct((depth, rows, n), F32),
        compiler_params=_cparams(("parallel", "parallel"), 40),
        name="ada",
    )(cv, w_ada, b_ada.reshape(depth, 1, n))


def _store_ones_padded(v_ref, rows, vals):
    lo = lax.broadcasted_iota(jnp.int32, (vals.shape[0], LANES), 1) < HDIM
    for t in range(vals.shape[1] // LANES):
        vt = vals[:, t * LANES:(t + 1) * LANES]
        v_ref[0, rows, (2 * t) * LANES:(2 * t + 1) * LANES] = jnp.where(lo, vt, 1.0).astype(BF16)
        v_ref[0, rows, (2 * t + 1) * LANES:(2 * t + 2) * LANES] = jnp.where(lo, 1.0, vt).astype(BF16)


def _inproj_ab_body(h_ref, sh_ref, sc_ref, g_ref, w_ref, ws_ref, bs_ref, gv_ref,
                    ya_ref, q_ref, k_ref, v_ref, *, sub, n_sub, aw):
    bw = q_ref.shape[-1]
    gd = aw // A_GROUPS
    bounds = [0, aw, 2 * aw, 2 * aw + bw, 2 * aw + 2 * bw, 2 * aw + 3 * bw]
    projs = []
    for t in range(n_sub):
        xm = _modulate(h_ref[0, t * sub:(t + 1) * sub], g_ref[0], sh_ref[0], sc_ref[0]).astype(BF16)
        projs.append([_dot(xm, w_ref[:, c0:c1]) for c0, c1 in zip(bounds[:-1], bounds[1:])])
    for t in range(n_sub):
        rows = slice(t * sub, (t + 1) * sub)
        tu, tva, tq, tk, tv = projs[t]
        q_ref[0, rows] = (tq * (HDIM ** -0.5 * LOG2E)).astype(BF16)
        k_ref[0, rows] = tk.astype(BF16)
        _store_ones_padded(v_ref, rows, tv)
        u = _gelu(tu)
        va = _gelu(tva)
        mu = jnp.mean(va, axis=-1, keepdims=True)
        d = va - mu
        var = jnp.mean(d * d, axis=-1, keepdims=True)
        vn = (d * lax.rsqrt(var + EPS) * gv_ref[...]).astype(BF16)
        for n in range(sub // A_CHUNK):
            r0, r1 = n * A_CHUNK, (n + 1) * A_CHUNK
            for g in range(A_GROUPS):
                c0, c1 = g * gd, (g + 1) * gd
                s = _dot(ws_ref[g], vn[r0:r1, c0:c1]) + bs_ref[g]
                ya_ref[0, t * sub + r0:t * sub + r1, c0:c1] = (u[r0:r1, c0:c1] * s).astype(BF16)


def _inproj_ab(h, sh, sc, g_pre, w_in, w_s, b_s_b, g_v, aw, bw):
    b, l, d = h.shape
    sub = _tile(l, 512)
    n_sub = 2 if l % (2 * sub) == 0 else 1
    tm = sub * n_sub
    n_in = w_in.shape[1]
    row = lambda bi, i: (bi, i, 0)
    c2 = lambda bi, i: (0, 0)
    c3 = lambda bi, i: (0, 0, 0)
    body = functools.partial(_inproj_ab_body, sub=sub, n_sub=n_sub, aw=aw)
    widths = (aw, bw, bw, 2 * bw)
    return pl.pallas_call(
        body,
        grid=(b, l // tm),
        in_specs=[pl.BlockSpec((1, tm, d), row), _vec_spec(sh), _vec_spec(sc), _vec_spec(g_pre),
                  pl.BlockSpec((d, n_in), c2),
                  pl.BlockSpec(w_s.shape, c3), pl.BlockSpec(b_s_b.shape, c3),
                  pl.BlockSpec((1, aw), c2)],
        out_specs=[pl.BlockSpec((1, tm, w), row) for w in widths],
        out_shape=[jax.ShapeDtypeStruct((b, l, w), BF16) for w in widths],
        compiler_params=_cparams(("parallel", "parallel"), 48),
        name="inproj_ab",
    )(h, sh[0], sc[0], g_pre[0], w_in, w_s, b_s_b, g_v)


def _softmax_pv(scores, values):
    m = scores[0].max(axis=-1, keepdims=True)
    for s in scores[1:]:
        m = jnp.maximum(m, s.max(axis=-1, keepdims=True))
    o = _dot(jnp.exp2(scores[0] - m).astype(BF16), values[0])
    for s, v in zip(scores[1:], values[1:]):
        o = o + _dot(jnp.exp2(s - m).astype(BF16), v)
    return o


def _normalised_pair(o_even, o_odd):
    lo = lax.broadcasted_iota(jnp.int32, o_even.shape, 1) < HDIM
    return jnp.where(lo, o_even / pltpu.roll(o_even, HDIM, axis=1), o_odd / pltpu.roll(o_odd, HDIM, axis=1))


def _na_body(q_ref, k_ref, v_ref, kc_ref, vc_ref, t_ref, o_ref, *, rows, n_blocks):
    for u in range(n_blocks):
        _na_block(q_ref, k_ref, v_ref, kc_ref, vc_ref, t_ref, o_ref,
                  n_blocks * pl.program_id(1) + u, u * NA_QROWS * GRID_W, rows)


def _na_block(q_ref, k_ref, v_ref, kc_ref, vc_ref, t_ref, o_ref, rb, qoff, rows):
    tq = NA_QROWS * GRID_W
    nk = NA_KROWS * GRID_W
    start = jnp.clip(NA_QROWS * rb - NA_ROWS // 2, 0, rows - NA_KROWS)
    koff = pl.multiple_of(start * GRID_W, NA_QROWS * GRID_W)
    lo_q = lax.broadcasted_iota(jnp.int32, (tq, LANES), 1) < HDIM
    lo_b = lax.broadcasted_iota(jnp.int32, (GRID_W, LANES), 1) < GRID_W
    idx = []
    for ri in range(NA_QROWS):
        r = NA_QROWS * rb + ri
        r0 = jnp.clip(r - NA_ROWS // 2, 0, rows - NA_ROWS)
        row = []
        for j in range(NA_KROWS):
            kr = start + j
            valid = jnp.logical_and(kr >= r0, kr < r0 + NA_ROWS)
            row.append(jnp.where(valid, kr - r + NA_ROWS - 1, 2 * NA_ROWS - 1))
        idx.append(row)
    for p in range(q_ref.shape[-1] // LANES):
        cp = slice(p * LANES, (p + 1) * LANES)
        q = q_ref[0, qoff:qoff + tq, cp]
        zero = jnp.zeros_like(q)
        lhs = jnp.concatenate([jnp.where(lo_q, q, zero), jnp.where(lo_q, zero, q)], axis=0)
        bias = jnp.concatenate([
            jnp.concatenate([
                jnp.where(lo_b, t_ref[2 * p + a, idx[ri][2 * m]], t_ref[2 * p + a, idx[ri][2 * m + 1]])
                for m in range(NA_KROWS // 2)], axis=1)
            for a in range(2) for ri in range(NA_QROWS)], axis=0)
        s_loc = _dot_t(lhs, k_ref[0, pl.ds(koff, nk), cp]) + bias
        s_ctx = _dot_t(lhs, kc_ref[0, :, cp])
        m = jnp.maximum(s_loc.max(axis=-1, keepdims=True), s_ctx.max(axis=-1, keepdims=True))
        p_loc = jnp.exp2(s_loc - m).astype(BF16)
        p_ctx = jnp.exp2(s_ctx - m).astype(BF16)
        outs = []
        for a in range(2):
            ch = slice((2 * p + a) * LANES, (2 * p + a + 1) * LANES)
            rs = slice(a * tq, (a + 1) * tq)
            outs.append(_dot(p_loc[rs], v_ref[0, pl.ds(koff, nk), ch]) + _dot(p_ctx[rs], vc_ref[0, :, ch]))
        o_ref[0, qoff:qoff + tq, cp] = _normalised_pair(outs[0], outs[1]).astype(BF16)


def _na_attention(q, k, v, kc, vc, tdup):
    b, s, w = q.shape
    lc = kc.shape[1]
    rows = s // GRID_W
    n_blocks = 8
    assert rows % (n_blocks * NA_QROWS) == 0
    tq = n_blocks * NA_QROWS * GRID_W
    body = functools.partial(_na_body, rows=rows, n_blocks=n_blocks)
    whole = lambda n, wd: pl.BlockSpec((1, n, wd), lambda bi, r: (bi, 0, 0))
    return pl.pallas_call(
        body,
        grid=(b, rows // (n_blocks * NA_QROWS)),
        in_specs=[pl.BlockSpec((1, tq, w), lambda bi, r: (bi, r, 0)),
                  whole(s, w), whole(s, 2 * w), whole(lc, w), whole(lc, 2 * w),
                  pl.BlockSpec(tdup.shape, lambda bi, r: (0, 0, 0, 0))],
        out_specs=pl.BlockSpec((1, tq, w), lambda bi, r: (bi, r, 0)),
        out_shape=jax.ShapeDtypeStruct((b, s, w), BF16),
        compiler_params=_cparams(("parallel", "arbitrary"), 48),
        name="na_attention",
    )(q, k, v, kc, vc, tdup)


def _ctx_attn_body(q_ref, k_ref, v_ref, o_ref):
    lo = lax.broadcasted_iota(jnp.int32, (q_ref.shape[1], LANES), 1) < HDIM
    for p in range(q_ref.shape[-1] // LANES):
        cp = slice(p * LANES, (p + 1) * LANES)
        q = q_ref[0, :, cp]
        k = k_ref[0, :, cp]
        outs = []
        for a in range(2):
            ch = slice((2 * p + a) * LANES, (2 * p + a + 1) * LANES)
            qa = jnp.where(lo if a == 0 else jnp.logical_not(lo), q, jnp.zeros_like(q))
            outs.append(_softmax_pv([_dot_t(qa, k)], [v_ref[0, :, ch]]))
        o_ref[0, :, cp] = _normalised_pair(outs[0], outs[1]).astype(BF16)


def _ctx_attention(q, k, v):
    b, l, w = q.shape
    spec = pl.BlockSpec((1, l, w), lambda bi: (bi, 0, 0))
    return pl.pallas_call(
        _ctx_attn_body,
        grid=(b,),
        in_specs=[spec, spec, pl.BlockSpec((1, l, 2 * w), lambda bi: (bi, 0, 0))],
        out_specs=spec,
        out_shape=jax.ShapeDtypeStruct((b, l, w), BF16),
        compiler_params=_cparams(("parallel",), 32),
        name="ctx_attention",
    )(q, k, v)


def _outproj_body(*refs, n_parts):
    y_refs = refs[:n_parts]
    w_refs = refs[n_parts:2 * n_parts]
    h_ref, gt_ref, g_ref, o_ref = refs[2 * n_parts:]
    y = _dot(y_refs[0][0], w_refs[0][...])
    for yr, wr in zip(y_refs[1:], w_refs[1:]):
        y = y + _dot(yr[0], wr[...])
    o_ref[0] = h_ref[0] + gt_ref[0] * _rms(y, g_ref[0])


def _outproj(ys, ws, h, gt, g_post):
    b, l, d = h.shape
    tm = _tile(l, 1024)
    row = lambda bi, i: (bi, i, 0)
    c2 = lambda bi, i: (0, 0)
    body = functools.partial(_outproj_body, n_parts=len(ys))
    return pl.pallas_call(
        body,
        grid=(b, l // tm),
        in_specs=([pl.BlockSpec((1, tm, y.shape[-1]), row) for y in ys]
                  + [pl.BlockSpec(w.shape, c2) for w in ws]
                  + [pl.BlockSpec((1, tm, d), row), _vec_spec(gt), _vec_spec(g_post)]),
        out_specs=pl.BlockSpec((1, tm, d), row),
        out_shape=jax.ShapeDtypeStruct((b, l, d), F32),
        compiler_params=_cparams(("parallel", "parallel"), 40),
        name="outproj",
    )(*ys, *ws, h, gt[0], g_post[0])


def _ffn_body(xp_ref, x_ref, xn_ref, sh_ref, sc_ref, gt_ref, gpre_ref, gpost_ref,
              wu_ref, cw_ref, cb_ref, wd_ref,
              o_ref, xs_ref, p3_ref, act_ref, acc_ref, *, tm, n_tiles, n_chunks, groups, seq):
    i = pl.program_id(1)
    g = gpre_ref[0]
    sh = sh_ref[0]
    sc = sc_ref[0]
    nv = tm // SUBLANES
    nl = x_ref.shape[-1] // LANES
    xm = _modulate(x_ref[0], g, sh, sc)
    for j in range(nl):
        for s in range(SUBLANES):
            p3_ref[j, pl.ds(s, nv, stride=SUBLANES), :] = xm[s * nv:(s + 1) * nv, j * LANES:(j + 1) * LANES]
    xs_ref[0:tm] = jnp.concatenate([p3_ref[j] for j in range(nl)], axis=1).astype(BF16)
    rid = lax.broadcasted_iota(jnp.int32, (2 * SUBLANES, 1), 0)
    keep = jnp.where(rid == 0, (i > 0).astype(F32), jnp.where(rid == 1, (i < n_tiles - 1).astype(F32), 0.0))
    halo = jnp.concatenate([xp_ref[0, SUBLANES - 1:SUBLANES], xn_ref[0, 0:1],
                            jnp.zeros((2 * SUBLANES - 2, x_ref.shape[-1]), F32)], axis=0)
    xs_ref[tm:tm + 2 * SUBLANES] = (_modulate(halo, g, sh, sc) * keep).astype(BF16)

    srow = lax.broadcasted_iota(jnp.int32, (SUBLANES, FFN_CW), 0)
    inner_start = functools.reduce(jnp.logical_or, [srow == s for s in range(1, SUBLANES) if (s * nv) % seq == 0],
                                   srow < 0)
    inner_end = functools.reduce(jnp.logical_or,
                                 [srow == s for s in range(SUBLANES - 1) if ((s + 1) * nv) % seq == 0], srow < 0)

    def conv(hh, cw, cb):
        hm = hh[0:tm]
        first = jnp.where(srow == 0, hh[tm:tm + 1], pltpu.roll(hm[tm - SUBLANES:tm], 1, axis=0))
        last = jnp.where(srow == SUBLANES - 1, hh[tm + 1:tm + 2], pltpu.roll(hm[0:SUBLANES], SUBLANES - 1, axis=0))
        first = jnp.where(inner_start, 0.0, first)
        last = jnp.where(inner_end, 0.0, last)
        prev = jnp.concatenate([first, hm[0:tm - SUBLANES]], axis=0)
        nxt = jnp.concatenate([hm[SUBLANES:tm], last], axis=0)
        return prev * cw[0:1] + hm * cw[1:2] + nxt * cw[2:3] + cb

    ff = n_chunks * FFN_CW

    def up(c):
        xs = xs_ref[...]
        return (_dot(xs, wu_ref[:, c * FFN_CW:(c + 1) * FFN_CW]),
                _dot(xs, wu_ref[:, ff + c * FFN_CW:ff + (c + 1) * FFN_CW]))

    pending = up(0)
    k0 = 0
    for c in range(n_chunks):
        ha, hg = pending
        if c + 1 < n_chunks:
            pending = up(c + 1)
        cc = slice(c * FFN_CW, (c + 1) * FFN_CW)
        cg = slice(ff + c * FFN_CW, ff + (c + 1) * FFN_CW)
        a = conv(ha, cw_ref[:, cc], cb_ref[:, cc])
        gg = conv(hg, cw_ref[:, cg], cb_ref[:, cg])
        act_ref[:, cc] = (_silu(gg) * a).astype(BF16)
        if c + 1 in groups:
            k1 = (c + 1) * FFN_CW
            part = _dot(act_ref[:, k0:k1], wd_ref[k0:k1, :])
            if k0 == 0:
                acc_ref[...] = part
            else:
                acc_ref[...] += part
            k0 = k1
    f = gt_ref[0] * _rms(acc_ref[...], gpost_ref[0])
    for j in range(nl):
        p3_ref[j] = f[:, j * LANES:(j + 1) * LANES]
    for s in range(SUBLANES):
        fs = jnp.concatenate([p3_ref[j, pl.ds(s, nv, stride=SUBLANES), :] for j in range(nl)], axis=1)
        o_ref[0, s * nv:(s + 1) * nv, :] = x_ref[0, s * nv:(s + 1) * nv, :] + fs


def _ffn(h, sh, sc, gt, g_pre, g_post, wts, seq_len=None):
    wu, cw, cb, wd = wts
    b, l, d = h.shape
    tm = _tile(l, 1024)
    n_tiles = l // tm
    seq = l if seq_len is None else seq_len
    assert seq == l or (l == tm and l % seq == 0 and seq % (tm // SUBLANES) == 0)
    ff = wd.shape[0]
    n_chunks = ff // FFN_CW
    groups = (6, n_chunks)
    hb = tm // SUBLANES
    last_hb = l // SUBLANES - 1
    row = lambda bi, i: (bi, i, 0)
    c2 = lambda bi, i: (0, 0)
    whole = lambda a: pl.BlockSpec(a.shape, lambda bi, i: (0,) * a.ndim, pipeline_mode=pl.Buffered(1))
    body = functools.partial(_ffn_body, tm=tm, n_tiles=n_tiles, n_chunks=n_chunks, groups=groups, seq=seq)
    return pl.pallas_call(
        body,
        grid=(b, n_tiles),
        in_specs=[pl.BlockSpec((1, SUBLANES, d), lambda bi, i: (bi, jnp.maximum(i * hb - 1, 0), 0)),
                  pl.BlockSpec((1, tm, d), row),
                  pl.BlockSpec((1, SUBLANES, d), lambda bi, i: (bi, jnp.minimum((i + 1) * hb, last_hb), 0)),
                  _vec_spec(sh), _vec_spec(sc), _vec_spec(gt), _vec_spec(g_pre), _vec_spec(g_post),
                  whole(wu), whole(cw), whole(cb), whole(wd)],
        out_specs=pl.BlockSpec((1, tm, d), row),
        out_shape=jax.ShapeDtypeStruct((b, l, d), F32),
        scratch_shapes=[pltpu.VMEM((tm + 2 * SUBLANES, d), BF16), pltpu.VMEM((d // LANES, tm, LANES), F32),
                        pltpu.VMEM((tm, ff), BF16), pltpu.VMEM((tm, d), F32)],
        compiler_params=_cparams(("parallel", "parallel"), 56),
        name="conv_ffn",
    )(h, h, h, sh[0], sc[0], gt[0], g_pre[0], g_post[0], wu, cw, cb, wd)


def _ffn_weights(w_up, conv_w, conv_b, w_down):
    return w_up.astype(BF16), conv_w, conv_b[None, :], w_down.astype(BF16)


def _inproj_c_body(h_ref, sh_ref, sc_ref, g_ref, wq_ref, wk_ref, wv_ref, gq_ref, gk_ref, bd_ref,
                   cos_ref, sin_ref, *out_refs, n_qtiles, sub, n_sub):
    bd = bd_ref[...]
    if n_qtiles:
        q_ref, k_ref, v_ref = out_refs
    else:
        k_ref, v_ref = out_refs
    projs = []
    for t in range(n_sub):
        xm = _modulate(h_ref[0, t * sub:(t + 1) * sub], g_ref[0], sh_ref[0], sc_ref[0]).astype(BF16)
        tq = _dot(xm, wq_ref[...]) if n_qtiles else None
        projs.append((tq, _dot(xm, wk_ref[...]), _dot(xm, wv_ref[...])))
    for t in range(n_sub):
        rows = slice(t * sub, (t + 1) * sub)
        tq, tk, tv = projs[t]
        _store_ones_padded(v_ref, rows, tv)
        tiles = [(tq[:, 2 * LANES * j:2 * LANES * j + LANES], tq[:, 2 * LANES * j + LANES:2 * LANES * (j + 1)])
                 for j in range(n_qtiles)] + [(tk[:, :LANES], tk[:, LANES:])]
        sq = jnp.concatenate([(t1 * t1 + t2 * t2).astype(BF16) for t1, t2 in tiles], axis=0)
        ss = _dot(sq, bd)
        cs = cos_ref[rows]
        sn = sin_ref[rows]
        for j, (t1, t2) in enumerate(tiles):
            is_q = j < n_qtiles
            o_ref = q_ref if is_q else k_ref
            gains = gq_ref[...] if is_q else gk_ref[...]
            c0 = 2 * LANES * j if is_q else 0
            r = lax.rsqrt(ss[j * sub:(j + 1) * sub] * (1.0 / HDIM) + EPS)
            a1 = t1 * r * gains[0:1]
            a2 = t2 * r * gains[1:2]
            o_ref[0, rows, c0:c0 + LANES] = (a1 * cs - a2 * sn).astype(BF16)
            o_ref[0, rows, c0 + LANES:c0 + 2 * LANES] = (a1 * sn + a2 * cs).astype(BF16)


def _inproj_c(h, sh, sc, g_pre, wq, wk, wv, gq, gk, bd, cos, sin, with_q):
    b, l, d = h.shape
    sub = _tile(l, 512)
    n_sub = 2 if l % (2 * sub) == 0 else 1
    tm = sub * n_sub
    row = lambda bi, i: (bi, i, 0)
    c2 = lambda bi, i: (0, 0)
    nq, nkv = wq.shape[1], wk.shape[1]
    n_qtiles = nq // (2 * LANES) if with_q else 0
    body = functools.partial(_inproj_c_body, n_qtiles=n_qtiles, sub=sub, n_sub=n_sub)
    out_specs = [pl.BlockSpec((1, tm, nkv), row), pl.BlockSpec((1, tm, 2 * nkv), row)]
    out_shape = [jax.ShapeDtypeStruct((b, l, nkv), BF16), jax.ShapeDtypeStruct((b, l, 2 * nkv), BF16)]
    if with_q:
        out_specs = [pl.BlockSpec((1, tm, nq), row)] + out_specs
        out_shape = [jax.ShapeDtypeStruct((b, l, nq), BF16)] + out_shape
    return pl.pallas_call(
        body,
        grid=(b, l // tm),
        in_specs=[pl.BlockSpec((1, tm, d), row), _vec_spec(sh), _vec_spec(sc), _vec_spec(g_pre),
                  pl.BlockSpec(wq.shape, c2), pl.BlockSpec(wk.shape, c2), pl.BlockSpec(wv.shape, c2),
                  pl.BlockSpec(gq.shape, c2), pl.BlockSpec(gk.shape, c2), pl.BlockSpec(bd.shape, c2),
                  pl.BlockSpec((tm, LANES), lambda bi, i: (i, 0)),
                  pl.BlockSpec((tm, LANES), lambda bi, i: (i, 0))],
        out_specs=out_specs,
        out_shape=out_shape,
        compiler_params=_cparams(("parallel", "parallel"), 40),
        name="inproj_c",
    )(h, sh[0], sc[0], g_pre[0], wq, wk, wv, gq, gk, bd, cos, sin)


def _gqa_body(q_ref, k_ref, v_ref, kc_ref, vc_ref, o_ref, *, tq, n_sub):
    k = k_ref[0]
    kc = kc_ref[0]
    seg = (lax.broadcasted_iota(jnp.int32, (tq, 2 * LANES), 1) % LANES) // (HDIM // 2)

    def scores(t):
        q = q_ref[0, t * tq:(t + 1) * tq]
        zero = jnp.zeros_like(q)
        lhs = jnp.concatenate([jnp.where(seg == s, q, zero) for s in range(4)], axis=0)
        return _dot_t(lhs, k), _dot_t(lhs, kc)

    pending = scores(0)
    for t in range(n_sub):
        s_lat, s_ctx = pending
        if t + 1 < n_sub:
            pending = scores(t + 1)
        m = jnp.maximum(s_lat.max(axis=-1, keepdims=True), s_ctx.max(axis=-1, keepdims=True))
        p_lat = jnp.exp2(s_lat - m).astype(BF16)
        p_ctx = jnp.exp2(s_ctx - m).astype(BF16)
        outs = []
        for s in range(4):
            rs = slice(s * tq, (s + 1) * tq)
            cs = slice(s * LANES, (s + 1) * LANES)
            outs.append(_dot(p_lat[rs], v_ref[0, :, cs]) + _dot(p_ctx[rs], vc_ref[0, :, cs]))
        o_ref[0, t * tq:(t + 1) * tq] = jnp.concatenate(
            [_normalised_pair(outs[0], outs[1]), _normalised_pair(outs[2], outs[3])], axis=1).astype(BF16)


def _gqa_attention(q, k, v, kc, vc):
    b, s, nq = q.shape
    lc = kc.shape[1]
    nkv = k.shape[-1]
    n_sub = 4
    tq = 256 * n_sub
    body = functools.partial(_gqa_body, tq=tq // n_sub, n_sub=n_sub)
    kv_spec = lambda n, w: pl.BlockSpec((1, n, w), lambda bi, i, j: (bi, 0, 0))
    return pl.pallas_call(
        body,
        grid=(b, s // tq, nq // (2 * LANES)),
        in_specs=[pl.BlockSpec((1, tq, 2 * LANES), lambda bi, i, j: (bi, i, j)),
                  kv_spec(s, nkv), kv_spec(s, 2 * nkv), kv_spec(lc, nkv), kv_spec(lc, 2 * nkv)],
        out_specs=pl.BlockSpec((1, tq, 2 * LANES), lambda bi, i, j: (bi, i, j)),
        out_shape=jax.ShapeDtypeStruct((b, s, nq), BF16),
        compiler_params=_cparams(("parallel", "arbitrary", "arbitrary"), 56),
        name="gqa_attention",
    )(q, k, v, kc, vc)


def _na_bias_table(rpb):
    qc = np.arange(GRID_W)[:, None]
    kc = np.arange(GRID_W)[None, :]
    c0 = np.clip(qc - NA_COLS // 2, 0, GRID_W - NA_COLS)
    in_win = (kc >= c0) & (kc < c0 + NA_COLS)
    n_h, n_dr, n_dc = rpb.shape
    period = 2 * GRID_W
    lead = GRID_W - NA_COLS
    sig = jnp.pad(rpb, ((0, 0), (0, 0), (lead, period - n_dc - lead)))
    toep = jnp.tile(sig, (1, 1, GRID_W))[..., :GRID_W * (period - 1)].reshape(n_h, n_dr, GRID_W, period - 1)
    toep = toep[..., GRID_W - 1:]
    t = jnp.where(in_win, toep * LOG2E, NEG_INF)
    t = jnp.concatenate([t, jnp.full_like(t[:, :1], NEG_INF)], axis=1)
    return jnp.concatenate([t, t], axis=-1)


def _rope_tables(n_tokens):
    t = np.arange(n_tokens)
    quarter = HDIM // 4
    inv = (ROPE_THETA ** (-np.arange(quarter, dtype=np.float32) / quarter)).astype(np.float32)
    rows = (t // GRID_W).astype(np.float32)[:, None] * inv
    cols = (t % GRID_W).astype(np.float32)[:, None] * inv
    ang = np.tile(np.concatenate([rows, cols], axis=-1), (1, LANES // (HDIM // 2)))
    return jnp.asarray(np.cos(ang), F32), jnp.asarray(np.sin(ang), F32)


def kernel(x, c, ctx, c_ctx, w_ada, b_ada, norm_g, w_in_ab, a_w_s, a_b_s, a_v_g, b_rpb, w_out_ab,
           w_qkv_c, c_q_g, c_k_g, w_out_c, w_up, conv_w, conv_b, w_down):
    bsz, seq, d = x.shape
    lc = ctx.shape[1]
    depth = w_ada.shape[0]
    aw = a_v_g.shape[-1]
    n_heads_b = b_rpb.shape[1]
    bw = n_heads_b * HDIM
    n_kv = 4
    n_heads_c = w_out_c.shape[1] // HDIM
    assert seq % (NA_QROWS * GRID_W) == 0 and seq // GRID_W >= NA_KROWS
    assert aw // A_GROUPS == A_CHUNK and w_up.shape[-1] // 2 % FFN_CW == 0

    n_rows = 8 * ((bsz + 1 + 7) // 8)
    cv = jnp.concatenate([c, c_ctx[None], jnp.zeros((n_rows - bsz - 1, d), F32)], axis=0)
    mods = _ada(cv, w_ada, b_ada).reshape(depth * n_rows * 6, 1, d)
    gains = norm_g.reshape(depth * 4, 1, d)

    def mod_vec(layer, j, is_ctx):
        base = layer * n_rows * 6 + j
        return mods, ((lambda bi: base + bsz * 6) if is_ctx else (lambda bi: base + bi * 6))

    def gain_vec(layer, j):
        return gains, (lambda bi: layer * 4 + j)

    merge = 2 if bsz % 2 == 0 and lc % A_CHUNK == 0 else 1
    merged = lambda a: a.reshape(bsz // merge, merge * lc, a.shape[-1])
    per_batch = lambda a: a.reshape(bsz, lc, a.shape[-1])

    cos_l, sin_l = _rope_tables(seq)
    cos_c, sin_c = jnp.ones((merge * lc, LANES), F32), jnp.zeros((merge * lc, LANES), F32)

    h, hc = x, merged(ctx)
    for layer in range(depth):
        with_ctx = layer < depth - 1
        lat = [mod_vec(layer, j, False) for j in range(6)]
        cx = [mod_vec(layer, j, True) for j in range(6)]
        g_pre_m, g_post_m, g_pre_f, g_post_f = [gain_vec(layer, j) for j in range(4)]
        if layer % 2 == 0:
            e = layer // 2
            w_in = w_in_ab[e].astype(BF16)
            w_s = a_w_s[e].astype(BF16)
            b_s_b = jnp.broadcast_to(a_b_s[e][:, :, None], a_w_s[e].shape[:2] + (aw // A_GROUPS,))
            g_v = a_v_g[e][None, :]
            w_out = w_out_ab[e].astype(BF16)
            ya, q, k, v = _inproj_ab(h, lat[0], lat[1], g_pre_m, w_in, w_s, b_s_b, g_v, aw, bw)
            yca, qc, kc, vc = _inproj_ab(hc, cx[0], cx[1], g_pre_m, w_in, w_s, b_s_b, g_v, aw, bw)
            qc, kc, vc = per_batch(qc), per_batch(kc), per_batch(vc)
            yb = _na_attention(q, k, v, kc, vc, _na_bias_table(b_rpb[e]))
            h = _outproj([ya, yb], [w_out[:aw], w_out[aw:]], h, lat[2], g_post_m)
            if with_ctx:
                ycb = merged(_ctx_attention(qc, kc, vc))
                hc = _outproj([yca, ycb], [w_out[:aw], w_out[aw:]], hc, cx[2], g_post_m)
        else:
            o = layer // 2
            nq = n_heads_c * HDIM
            nkv = n_kv * HDIM
            per = n_heads_c // n_kv
            half = HDIM // 2
            wqkv = w_qkv_c[o].astype(BF16)
            wq = wqkv[:, :nq].reshape(d, n_kv, per, 2, half).transpose(0, 2, 3, 1, 4).reshape(d, nq)
            wk = wqkv[:, nq:nq + nkv].reshape(d, n_kv, 2, half).transpose(0, 2, 1, 3).reshape(d, nkv)
            wv = wqkv[:, nq + nkv:]
            reps = LANES // half
            gq = (jnp.stack([jnp.tile(c_q_g[o][:half], reps), jnp.tile(c_q_g[o][half:], reps)])
                  * (HDIM ** -0.5 * LOG2E))
            gk = jnp.stack([jnp.tile(c_k_g[o][:half], reps), jnp.tile(c_k_g[o][half:], reps)])
            seg = np.arange(LANES) // half
            bd = jnp.asarray(seg[:, None] == seg[None, :], BF16)
            w_out = w_out_c[o].astype(BF16).reshape(n_kv, per, HDIM, d).transpose(1, 0, 2, 3).reshape(nq, d)
            q, k, v = _inproj_c(h, lat[0], lat[1], g_pre_m, wq, wk, wv, gq, gk, bd, cos_l, sin_l, True)
            kc, vc = _inproj_c(hc, cx[0], cx[1], g_pre_m, wq, wk, wv, gq, gk, bd, cos_c, sin_c, False)
            y = _gqa_attention(q, k, v, per_batch(kc), per_batch(vc))
            h = _outproj([y], [w_out], h, lat[2], g_post_m)
            if with_ctx:
                raise NotImplementedError("context update after a grouped-query layer")
        wts = _ffn_weights(w_up[layer], conv_w[layer], conv_b[layer], w_down[layer])
        h = _ffn(h, lat[3], lat[4], lat[5], g_pre_f, g_post_f, wts)
        if with_ctx:
            hc = _ffn(hc, cx[3], cx[4], cx[5], g_pre_f, g_post_f, wts, seq_len=lc)
    return h
```

```python
import functools

import numpy as np
import jax
import jax.numpy as jnp
from jax import lax
from jax.experimental import pallas as pl
from jax.experimental.pallas import tpu as pltpu

F32 = jnp.float32
BF16 = jnp.bfloat16

EPS = 1e-6
NEG_INF = -1e30
GRID_W = 64
ROPE_THETA = 10000.0
LOG2E = 1.4426950408889634
A_CHUNK = 128
A_GROUPS = 4
NA_ROWS = 8
NA_COLS = 16
HDIM = 64
LANES = 128
NA_QROWS = 4
NA_KROWS = 12
SUBLANES = 8
FFN_CW = 256
V7X_VMEM_BYTES = 64 * 1024 * 1024


def _cparams(sem, vmem_mb):
    assert vmem_mb * 1024 * 1024 < V7X_VMEM_BYTES
    return pltpu.CompilerParams(dimension_semantics=sem, vmem_limit_bytes=vmem_mb * 1024 * 1024)


def _tile(n, pref):
    return pref if n % pref == 0 else n


def _vec_spec(vec):
    table, row = vec
    return pl.BlockSpec((1, 1, table.shape[-1]), lambda bi, *_: (row(bi), 0, 0))


def _rms(x, g):
    return x * lax.rsqrt(jnp.mean(x * x, axis=-1, keepdims=True) + EPS) * g


def _modulate(x, g, shift, scale):
    return _rms(x, g) * (1.0 + scale) + shift


def _gelu(x):
    return 0.5 * x * (1.0 + jnp.tanh(0.7978845608028654 * (x + 0.044715 * (x * x * x))))


def _silu(x):
    return x * jax.nn.sigmoid(x)


def _dot(a, b):
    return jnp.dot(a, b, preferred_element_type=F32)


def _dot_t(a, b):
    return lax.dot_general(a, b, (((1,), (1,)), ((), ())), preferred_element_type=F32)


def _ada_body(c_ref, w_ref, b_ref, o_ref):
    s = _silu(c_ref[...])
    o_ref[0] = jnp.dot(s, w_ref[0], preferred_element_type=F32,
                       precision=lax.Precision.HIGHEST) + b_ref[0]


def _ada(cv, w_ada, b_ada):
    depth, d, n = w_ada.shape
    rows = cv.shape[0]
    tn = _tile(n, 1536)
    return pl.pallas_call(
        _ada_body,
        grid=(depth, n // tn),
        in_specs=[pl.BlockSpec((rows, d), lambda l, j: (0, 0)),
                  pl.BlockSpec((1, d, tn), lambda l, j: (l, 0, j)),
                  pl.BlockSpec((1, 1, tn), lambda l, j: (l, 0, j))],
        out_specs=pl.BlockSpec((1, rows, tn), lambda l, j: (l, 0, j)),
        out_shape=jax.ShapeDtypeStruct((depth, rows, n), F32),
        compiler_params=_cparams(("parallel", "parallel"), 40),
        name="ada",
    )(cv, w_ada, b_ada.reshape(depth, 1, n))


def _store_ones_padded(v_ref, rows, vals):
    lo = lax.broadcasted_iota(jnp.int32, (vals.shape[0], LANES), 1) < HDIM
    for t in range(vals.shape[1] // LANES):
        vt = vals[:, t * LANES:(t + 1) * LANES]
        v_ref[0, rows, (2 * t) * LANES:(2 * t + 1) * LANES] = jnp.where(lo, vt, 1.0).astype(BF16)
        v_ref[0, rows, (2 * t + 1) * LANES:(2 * t + 2) * LANES] = jnp.where(lo, 1.0, vt).astype(BF16)


def _inproj_ab_body(h_ref, sh_ref, sc_ref, g_ref, w_ref, ws_ref, bs_ref, gv_ref,
                    ya_ref, q_ref, k_ref, v_ref, *, sub, n_sub, aw):
    bw = q_ref.shape[-1]
    gd = aw // A_GROUPS
    bounds = [0, aw, 2 * aw, 2 * aw + bw, 2 * aw + 2 * bw, 2 * aw + 3 * bw]
    projs = []
    for t in range(n_sub):
        xm = _modulate(h_ref[0, t * sub:(t + 1) * sub], g_ref[0], sh_ref[0], sc_ref[0]).astype(BF16)
        projs.append([_dot(xm, w_ref[:, c0:c1]) for c0, c1 in zip(bounds[:-1], bounds[1:])])
    for t in range(n_sub):
        rows = slice(t * sub, (t + 1) * sub)
        tu, tva, tq, tk, tv = projs[t]
        q_ref[0, rows] = (tq * (HDIM ** -0.5 * LOG2E)).astype(BF16)
        k_ref[0, rows] = tk.astype(BF16)
        _store_ones_padded(v_ref, rows, tv)
        u = _gelu(tu)
        va = _gelu(tva)
        mu = jnp.mean(va, axis=-1, keepdims=True)
        d = va - mu
        var = jnp.mean(d * d, axis=-1, keepdims=True)
        vn = (d * lax.rsqrt(var + EPS) * gv_ref[...]).astype(BF16)
        for n in range(sub // A_CHUNK):
            r0, r1 = n * A_CHUNK, (n + 1) * A_CHUNK
            for g in range(A_GROUPS):
                c0, c1 = g * gd, (g + 1) * gd
                s = _dot(ws_ref[g], vn[r0:r1, c0:c1]) + bs_ref[g]
                ya_ref[0, t * sub + r0:t * sub + r1, c0:c1] = (u[r0:r1, c0:c1] * s).astype(BF16)


def _inproj_ab(h, sh, sc, g_pre, w_in, w_s, b_s_b, g_v, aw, bw):
    b, l, d = h.shape
    sub = _tile(l, 512)
    n_sub = 2 if l % (2 * sub) == 0 else 1
    tm = sub * n_sub
    n_in = w_in.shape[1]
    row = lambda bi, i: (bi, i, 0)
    c2 = lambda bi, i: (0, 0)
    c3 = lambda bi, i: (0, 0, 0)
    body = functools.partial(_inproj_ab_body, sub=sub, n_sub=n_sub, aw=aw)
    widths = (aw, bw, bw, 2 * bw)
    return pl.pallas_call(
        body,
        grid=(b, l // tm),
        in_specs=[pl.BlockSpec((1, tm, d), row), _vec_spec(sh), _vec_spec(sc), _vec_spec(g_pre),
                  pl.BlockSpec((d, n_in), c2),
                  pl.BlockSpec(w_s.shape, c3), pl.BlockSpec(b_s_b.shape, c3),
                  pl.BlockSpec((1, aw), c2)],
        out_specs=[pl.BlockSpec((1, tm, w), row) for w in widths],
        out_shape=[jax.ShapeDtypeStruct((b, l, w), BF16) for w in widths],
        compiler_params=_cparams(("parallel", "parallel"), 48),
        name="inproj_ab",
    )(h, sh[0], sc[0], g_pre[0], w_in, w_s, b_s_b, g_v)


def _softmax_pv(scores, values):
    m = scores[0].max(axis=-1, keepdims=True)
    for s in scores[1:]:
        m = jnp.maximum(m, s.max(axis=-1, keepdims=True))
    o = _dot(jnp.exp2(scores[0] - m).astype(BF16), values[0])
    for s, v in zip(scores[1:], values[1:]):
        o = o + _dot(jnp.exp2(s - m).astype(BF16), v)
    return o


def _normalised_pair(o_even, o_odd):
    lo = lax.broadcasted_iota(jnp.int32, o_even.shape, 1) < HDIM
    return jnp.where(lo, o_even / pltpu.roll(o_even, HDIM, axis=1), o_odd / pltpu.roll(o_odd, HDIM, axis=1))


def _na_body(q_ref, k_ref, v_ref, kc_ref, vc_ref, t_ref, o_ref, *, rows, n_blocks):
    for u in range(n_blocks):
        _na_block(q_ref, k_ref, v_ref, kc_ref, vc_ref, t_ref, o_ref,
                  n_blocks * pl.program_id(1) + u, u * NA_QROWS * GRID_W, rows)


def _na_block(q_ref, k_ref, v_ref, kc_ref, vc_ref, t_ref, o_ref, rb, qoff, rows):
    tq = NA_QROWS * GRID_W
    nk = NA_KROWS * GRID_W
    start = jnp.clip(NA_QROWS * rb - NA_ROWS // 2, 0, rows - NA_KROWS)
    koff = pl.multiple_of(start * GRID_W, NA_QROWS * GRID_W)
    lo_q = lax.broadcasted_iota(jnp.int32, (tq, LANES), 1) < HDIM
    lo_b = lax.broadcasted_iota(jnp.int32, (GRID_W, LANES), 1) < GRID_W
    idx = []
    for ri in range(NA_QROWS):
        r = NA_QROWS * rb + ri
        r0 = jnp.clip(r - NA_ROWS // 2, 0, rows - NA_ROWS)
        row = []
        for j in range(NA_KROWS):
            kr = start + j
            valid = jnp.logical_and(kr >= r0, kr < r0 + NA_ROWS)
            row.append(jnp.where(valid, kr - r + NA_ROWS - 1, 2 * NA_ROWS - 1))
        idx.append(row)
    for p in range(q_ref.shape[-1] // LANES):
        cp = slice(p * LANES, (p + 1) * LANES)
        q = q_ref[0, qoff:qoff + tq, cp]
        zero = jnp.zeros_like(q)
        lhs = jnp.concatenate([jnp.where(lo_q, q, zero), jnp.where(lo_q, zero, q)], axis=0)
        bias = jnp.concatenate([
            jnp.concatenate([
                jnp.where(lo_b, t_ref[2 * p + a, idx[ri][2 * m]], t_ref[2 * p + a, idx[ri][2 * m + 1]])
                for m in range(NA_KROWS // 2)], axis=1)
            for a in range(2) for ri in range(NA_QROWS)], axis=0)
        s_loc = _dot_t(lhs, k_ref[0, pl.ds(koff, nk), cp]) + bias
        s_ctx = _dot_t(lhs, kc_ref[0, :, cp])
        m = jnp.maximum(s_loc.max(axis=-1, keepdims=True), s_ctx.max(axis=-1, keepdims=True))
        p_loc = jnp.exp2(s_loc - m).astype(BF16)
        p_ctx = jnp.exp2(s_ctx - m).astype(BF16)
        outs = []
        for a in range(2):
            ch = slice((2 * p + a) * LANES, (2 * p + a + 1) * LANES)
            rs = slice(a * tq, (a + 1) * tq)
            outs.append(_dot(p_loc[rs], v_ref[0, pl.ds(koff, nk), ch]) + _dot(p_ctx[rs], vc_ref[0, :, ch]))
        o_ref[0, qoff:qoff + tq, cp] = _normalised_pair(outs[0], outs[1]).astype(BF16)


def _na_attention(q, k, v, kc, vc, tdup):
    b, s, w = q.shape
    lc = kc.shape[1]
    rows = s // GRID_W
    n_blocks = 8
    assert rows % (n_blocks * NA_QROWS) == 0
    tq = n_blocks * NA_QROWS * GRID_W
    body = functools.partial(_na_body, rows=rows, n_blocks=n_blocks)
    whole = lambda n, wd: pl.BlockSpec((1, n, wd), lambda bi, r: (bi, 0, 0))
    return pl.pallas_call(
        body,
        grid=(b, rows // (n_blocks * NA_QROWS)),
        in_specs=[pl.BlockSpec((1, tq, w), lambda bi, r: (bi, r, 0)),
                  whole(s, w), whole(s, 2 * w), whole(lc, w), whole(lc, 2 * w),
                  pl.BlockSpec(tdup.shape, lambda bi, r: (0, 0, 0, 0))],
        out_specs=pl.BlockSpec((1, tq, w), lambda bi, r: (bi, r, 0)),
        out_shape=jax.ShapeDtypeStruct((b, s, w), BF16),
        compiler_params=_cparams(("parallel", "arbitrary"), 48),
        name="na_attention",
    )(q, k, v, kc, vc, tdup)


def _ctx_attn_body(q_ref, k_ref, v_ref, ya_ref, wa_ref, wb_ref, h_ref, gt_ref, g_ref, o_ref):
    lo = lax.broadcasted_iota(jnp.int32, (q_ref.shape[1], LANES), 1) < HDIM
    tiles = []
    for p in range(q_ref.shape[-1] // LANES):
        cp = slice(p * LANES, (p + 1) * LANES)
        q = q_ref[0, :, cp]
        k = k_ref[0, :, cp]
        outs = []
        for a in range(2):
            ch = slice((2 * p + a) * LANES, (2 * p + a + 1) * LANES)
            qa = jnp.where(lo if a == 0 else jnp.logical_not(lo), q, jnp.zeros_like(q))
            outs.append(_softmax_pv([_dot_t(qa, k)], [v_ref[0, :, ch]]))
        tiles.append(_normalised_pair(outs[0], outs[1]).astype(BF16))
    y = _dot(ya_ref[0], wa_ref[...]) + _dot(jnp.concatenate(tiles, axis=1), wb_ref[...])
    o_ref[0] = h_ref[0] + gt_ref[0] * _rms(y, g_ref[0])


def _ctx_attention_outproj(q, k, v, ya, wa, wb, h, gt, g_post):
    b, l, w = q.shape
    d = h.shape[-1]
    spec = lambda n: pl.BlockSpec((1, l, n), lambda bi: (bi, 0, 0))
    c2 = lambda bi: (0, 0)
    return pl.pallas_call(
        _ctx_attn_body,
        grid=(b,),
        in_specs=[spec(w), spec(w), spec(2 * w), spec(ya.shape[-1]),
                  pl.BlockSpec(wa.shape, c2), pl.BlockSpec(wb.shape, c2), spec(d),
                  _vec_spec(gt), _vec_spec(g_post)],
        out_specs=spec(d),
        out_shape=jax.ShapeDtypeStruct((b, l, d), F32),
        compiler_params=_cparams(("parallel",), 32),
        name="ctx_attention",
    )(q, k, v, ya, wa, wb, h, gt[0], g_post[0])


def _outproj_body(*refs, n_parts):
    y_refs = refs[:n_parts]
    w_refs = refs[n_parts:2 * n_parts]
    h_ref, gt_ref, g_ref, o_ref = refs[2 * n_parts:]
    y = _dot(y_refs[0][0], w_refs[0][...])
    for yr, wr in zip(y_refs[1:], w_refs[1:]):
        y = y + _dot(yr[0], wr[...])
    o_ref[0] = h_ref[0] + gt_ref[0] * _rms(y, g_ref[0])


def _outproj(ys, ws, h, gt, g_post):
    b, l, d = h.shape
    tm = _tile(l, 1024)
    row = lambda bi, i: (bi, i, 0)
    c2 = lambda bi, i: (0, 0)
    body = functools.partial(_outproj_body, n_parts=len(ys))
    return pl.pallas_call(
        body,
        grid=(b, l // tm),
        in_specs=([pl.BlockSpec((1, tm, y.shape[-1]), row) for y in ys]
                  + [pl.BlockSpec(w.shape, c2) for w in ws]
                  + [pl.BlockSpec((1, tm, d), row), _vec_spec(gt), _vec_spec(g_post)]),
        out_specs=pl.BlockSpec((1, tm, d), row),
        out_shape=jax.ShapeDtypeStruct((b, l, d), F32),
        compiler_params=_cparams(("parallel", "parallel"), 40),
        name="outproj",
    )(*ys, *ws, h, gt[0], g_post[0])


def _ffn_body(xp_ref, x_ref, xn_ref, sh_ref, sc_ref, gt_ref, gpre_ref, gpost_ref,
              wu_ref, cw_ref, cb_ref, wd_ref,
              o_ref, xs_ref, p3_ref, act_ref, acc_ref, *, tm, n_tiles, n_chunks, groups, seq):
    i = pl.program_id(1)
    g = gpre_ref[0]
    sh = sh_ref[0]
    sc = sc_ref[0]
    nv = tm // SUBLANES
    nl = x_ref.shape[-1] // LANES
    xm = _modulate(x_ref[0], g, sh, sc)
    for j in range(nl):
        for s in range(SUBLANES):
            p3_ref[j, pl.ds(s, nv, stride=SUBLANES), :] = xm[s * nv:(s + 1) * nv, j * LANES:(j + 1) * LANES]
    xs_ref[0:tm] = jnp.concatenate([p3_ref[j] for j in range(nl)], axis=1).astype(BF16)
    rid = lax.broadcasted_iota(jnp.int32, (2 * SUBLANES, 1), 0)
    keep = jnp.where(rid == 0, (i > 0).astype(F32), jnp.where(rid == 1, (i < n_tiles - 1).astype(F32), 0.0))
    halo = jnp.concatenate([xp_ref[0, SUBLANES - 1:SUBLANES], xn_ref[0, 0:1],
                            jnp.zeros((2 * SUBLANES - 2, x_ref.shape[-1]), F32)], axis=0)
    xs_ref[tm:tm + 2 * SUBLANES] = (_modulate(halo, g, sh, sc) * keep).astype(BF16)

    srow = lax.broadcasted_iota(jnp.int32, (SUBLANES, FFN_CW), 0)
    inner_start = functools.reduce(jnp.logical_or, [srow == s for s in range(1, SUBLANES) if (s * nv) % seq == 0],
                                   srow < 0)
    inner_end = functools.reduce(jnp.logical_or,
                                 [srow == s for s in range(SUBLANES - 1) if ((s + 1) * nv) % seq == 0], srow < 0)

    def conv(hh, cw, cb):
        hm = hh[0:tm]
        first = jnp.where(srow == 0, hh[tm:tm + 1], pltpu.roll(hm[tm - SUBLANES:tm], 1, axis=0))
        last = jnp.where(srow == SUBLANES - 1, hh[tm + 1:tm + 2], pltpu.roll(hm[0:SUBLANES], SUBLANES - 1, axis=0))
        first = jnp.where(inner_start, 0.0, first)
        last = jnp.where(inner_end, 0.0, last)
        prev = jnp.concatenate([first, hm[0:tm - SUBLANES]], axis=0)
        nxt = jnp.concatenate([hm[SUBLANES:tm], last], axis=0)
        return prev * cw[0:1] + hm * cw[1:2] + nxt * cw[2:3] + cb

    ff = n_chunks * FFN_CW

    def up(c):
        xs = xs_ref[...]
        return (_dot(xs, wu_ref[:, c * FFN_CW:(c + 1) * FFN_CW]),
                _dot(xs, wu_ref[:, ff + c * FFN_CW:ff + (c + 1) * FFN_CW]))

    pending = up(0)
    k0 = 0
    for c in range(n_chunks):
        ha, hg = pending
        if c + 1 < n_chunks:
            pending = up(c + 1)
        cc = slice(c * FFN_CW, (c + 1) * FFN_CW)
        cg = slice(ff + c * FFN_CW, ff + (c + 1) * FFN_CW)
        a = conv(ha, cw_ref[:, cc], cb_ref[:, cc])
        gg = conv(hg, cw_ref[:, cg], cb_ref[:, cg])
        act_ref[:, cc] = (_silu(gg) * a).astype(BF16)
        if c + 1 in groups:
            k1 = (c + 1) * FFN_CW
            part = _dot(act_ref[:, k0:k1], wd_ref[k0:k1, :])
            if k0 == 0:
                acc_ref[...] = part
            else:
                acc_ref[...] += part
            k0 = k1
    f = gt_ref[0] * _rms(acc_ref[...], gpost_ref[0])
    for j in range(nl):
        p3_ref[j] = f[:, j * LANES:(j + 1) * LANES]
    for s in range(SUBLANES):
        fs = jnp.concatenate([p3_ref[j, pl.ds(s, nv, stride=SUBLANES), :] for j in range(nl)], axis=1)
        o_ref[0, s * nv:(s + 1) * nv, :] = x_ref[0, s * nv:(s + 1) * nv, :] + fs


def _ffn(h, sh, sc, gt, g_pre, g_post, wts, seq_len=None):
    wu, cw, cb, wd = wts
    b, l, d = h.shape
    tm = _tile(l, 1024)
    n_tiles = l // tm
    seq = l if seq_len is None else seq_len
    assert seq == l or (l == tm and l % seq == 0 and seq % (tm // SUBLANES) == 0)
    ff = wd.shape[0]
    n_chunks = ff // FFN_CW
    groups = (6, n_chunks)
    hb = tm // SUBLANES
    last_hb = l // SUBLANES - 1
    row = lambda bi, i: (bi, i, 0)
    c2 = lambda bi, i: (0, 0)
    whole = lambda a: pl.BlockSpec(a.shape, lambda bi, i: (0,) * a.ndim, pipeline_mode=pl.Buffered(1))
    body = functools.partial(_ffn_body, tm=tm, n_tiles=n_tiles, n_chunks=n_chunks, groups=groups, seq=seq)
    return pl.pallas_call(
        body,
        grid=(b, n_tiles),
        in_specs=[pl.BlockSpec((1, SUBLANES, d), lambda bi, i: (bi, jnp.maximum(i * hb - 1, 0), 0)),
                  pl.BlockSpec((1, tm, d), row),
                  pl.BlockSpec((1, SUBLANES, d), lambda bi, i: (bi, jnp.minimum((i + 1) * hb, last_hb), 0)),
                  _vec_spec(sh), _vec_spec(sc), _vec_spec(gt), _vec_spec(g_pre), _vec_spec(g_post),
                  whole(wu), whole(cw), whole(cb), whole(wd)],
        out_specs=pl.BlockSpec((1, tm, d), row),
        out_shape=jax.ShapeDtypeStruct((b, l, d), F32),
        scratch_shapes=[pltpu.VMEM((tm + 2 * SUBLANES, d), BF16), pltpu.VMEM((d // LANES, tm, LANES), F32),
                        pltpu.VMEM((tm, ff), BF16), pltpu.VMEM((tm, d), F32)],
        compiler_params=_cparams(("parallel", "parallel"), 56),
        name="conv_ffn",
    )(h, h, h, sh[0], sc[0], gt[0], g_pre[0], g_post[0], wu, cw, cb, wd)


def _ffn_weights(w_up, conv_w, conv_b, w_down):
    return w_up.astype(BF16), conv_w, conv_b[None, :], w_down.astype(BF16)


def _inproj_c_body(h_ref, sh_ref, sc_ref, g_ref, wq_ref, wk_ref, wv_ref, gq_ref, gk_ref, bd_ref,
                   cos_ref, sin_ref, *out_refs, n_qtiles, sub, n_sub):
    bd = bd_ref[...]
    if n_qtiles:
        q_ref, k_ref, v_ref = out_refs
    else:
        k_ref, v_ref = out_refs
    projs = []
    for t in range(n_sub):
        xm = _modulate(h_ref[0, t * sub:(t + 1) * sub], g_ref[0], sh_ref[0], sc_ref[0]).astype(BF16)
        tq = _dot(xm, wq_ref[...]) if n_qtiles else None
        projs.append((tq, _dot(xm, wk_ref[...]), _dot(xm, wv_ref[...])))
    for t in range(n_sub):
        rows = slice(t * sub, (t + 1) * sub)
        tq, tk, tv = projs[t]
        _store_ones_padded(v_ref, rows, tv)
        tiles = [(tq[:, 2 * LANES * j:2 * LANES * j + LANES], tq[:, 2 * LANES * j + LANES:2 * LANES * (j + 1)])
                 for j in range(n_qtiles)] + [(tk[:, :LANES], tk[:, LANES:])]
        sq = jnp.concatenate([(t1 * t1 + t2 * t2).astype(BF16) for t1, t2 in tiles], axis=0)
        ss = _dot(sq, bd)
        cs = cos_ref[rows]
        sn = sin_ref[rows]
        for j, (t1, t2) in enumerate(tiles):
            is_q = j < n_qtiles
            o_ref = q_ref if is_q else k_ref
            gains = gq_ref[...] if is_q else gk_ref[...]
            c0 = 2 * LANES * j if is_q else 0
            r = lax.rsqrt(ss[j * sub:(j + 1) * sub] * (1.0 / HDIM) + EPS)
            a1 = t1 * r * gains[0:1]
            a2 = t2 * r * gains[1:2]
            o_ref[0, rows, c0:c0 + LANES] = (a1 * cs - a2 * sn).astype(BF16)
            o_ref[0, rows, c0 + LANES:c0 + 2 * LANES] = (a1 * sn + a2 * cs).astype(BF16)


def _inproj_c(h, sh, sc, g_pre, wq, wk, wv, gq, gk, bd, cos, sin, with_q):
    b, l, d = h.shape
    sub = _tile(l, 512)
    n_sub = 2 if l % (2 * sub) == 0 else 1
    tm = sub * n_sub
    row = lambda bi, i: (bi, i, 0)
    c2 = lambda bi, i: (0, 0)
    nq, nkv = wq.shape[1], wk.shape[1]
    n_qtiles = nq // (2 * LANES) if with_q else 0
    body = functools.partial(_inproj_c_body, n_qtiles=n_qtiles, sub=sub, n_sub=n_sub)
    out_specs = [pl.BlockSpec((1, tm, nkv), row), pl.BlockSpec((1, tm, 2 * nkv), row)]
    out_shape = [jax.ShapeDtypeStruct((b, l, nkv), BF16), jax.ShapeDtypeStruct((b, l, 2 * nkv), BF16)]
    if with_q:
        out_specs = [pl.BlockSpec((1, tm, nq), row)] + out_specs
        out_shape = [jax.ShapeDtypeStruct((b, l, nq), BF16)] + out_shape
    return pl.pallas_call(
        body,
        grid=(b, l // tm),
        in_specs=[pl.BlockSpec((1, tm, d), row), _vec_spec(sh), _vec_spec(sc), _vec_spec(g_pre),
                  pl.BlockSpec(wq.shape, c2), pl.BlockSpec(wk.shape, c2), pl.BlockSpec(wv.shape, c2),
                  pl.BlockSpec(gq.shape, c2), pl.BlockSpec(gk.shape, c2), pl.BlockSpec(bd.shape, c2),
                  pl.BlockSpec((tm, LANES), lambda bi, i: (i, 0)),
                  pl.BlockSpec((tm, LANES), lambda bi, i: (i, 0))],
        out_specs=out_specs,
        out_shape=out_shape,
        compiler_params=_cparams(("parallel", "parallel"), 40),
        name="inproj_c",
    )(h, sh[0], sc[0], g_pre[0], wq, wk, wv, gq, gk, bd, cos, sin)


def _gqa_body(q_ref, k_ref, v_ref, kc_ref, vc_ref, o_ref, *, tq, n_sub):
    k = k_ref[0]
    kc = kc_ref[0]
    seg = (lax.broadcasted_iota(jnp.int32, (tq, 2 * LANES), 1) % LANES) // (HDIM // 2)

    def scores(t):
        q = q_ref[0, t * tq:(t + 1) * tq]
        zero = jnp.zeros_like(q)
        lhs = jnp.concatenate([jnp.where(seg == s, q, zero) for s in range(4)], axis=0)
        return _dot_t(lhs, k), _dot_t(lhs, kc)

    pending = scores(0)
    for t in range(n_sub):
        s_lat, s_ctx = pending
        if t + 1 < n_sub:
            pending = scores(t + 1)
        m = jnp.maximum(s_lat.max(axis=-1, keepdims=True), s_ctx.max(axis=-1, keepdims=True))
        p_lat = jnp.exp2(s_lat - m).astype(BF16)
        p_ctx = jnp.exp2(s_ctx - m).astype(BF16)
        outs = []
        for s in range(4):
            rs = slice(s * tq, (s + 1) * tq)
            cs = slice(s * LANES, (s + 1) * LANES)
            outs.append(_dot(p_lat[rs], v_ref[0, :, cs]) + _dot(p_ctx[rs], vc_ref[0, :, cs]))
        o_ref[0, t * tq:(t + 1) * tq] = jnp.concatenate(
            [_normalised_pair(outs[0], outs[1]), _normalised_pair(outs[2], outs[3])], axis=1).astype(BF16)


def _gqa_attention(q, k, v, kc, vc):
    b, s, nq = q.shape
    lc = kc.shape[1]
    nkv = k.shape[-1]
    n_sub = 4
    tq = 256 * n_sub
    body = functools.partial(_gqa_body, tq=tq // n_sub, n_sub=n_sub)
    kv_spec = lambda n, w: pl.BlockSpec((1, n, w), lambda bi, i, j: (bi, 0, 0))
    return pl.pallas_call(
        body,
        grid=(b, s // tq, nq // (2 * LANES)),
        in_specs=[pl.BlockSpec((1, tq, 2 * LANES), lambda bi, i, j: (bi, i, j)),
                  kv_spec(s, nkv), kv_spec(s, 2 * nkv), kv_spec(lc, nkv), kv_spec(lc, 2 * nkv)],
        out_specs=pl.BlockSpec((1, tq, 2 * LANES), lambda bi, i, j: (bi, i, j)),
        out_shape=jax.ShapeDtypeStruct((b, s, nq), BF16),
        compiler_params=_cparams(("parallel", "arbitrary", "arbitrary"), 56),
        name="gqa_attention",
    )(q, k, v, kc, vc)


def _na_bias_table(rpb):
    qc = np.arange(GRID_W)[:, None]
    kc = np.arange(GRID_W)[None, :]
    c0 = np.clip(qc - NA_COLS // 2, 0, GRID_W - NA_COLS)
    in_win = (kc >= c0) & (kc < c0 + NA_COLS)
    n_h, n_dr, n_dc = rpb.shape
    period = 2 * GRID_W
    lead = GRID_W - NA_COLS
    sig = jnp.pad(rpb, ((0, 0), (0, 0), (lead, period - n_dc - lead)))
    toep = jnp.tile(sig, (1, 1, GRID_W))[..., :GRID_W * (period - 1)].reshape(n_h, n_dr, GRID_W, period - 1)
    toep = toep[..., GRID_W - 1:]
    t = jnp.where(in_win, toep * LOG2E, NEG_INF)
    t = jnp.concatenate([t, jnp.full_like(t[:, :1], NEG_INF)], axis=1)
    return jnp.concatenate([t, t], axis=-1)


def _rope_tables(n_tokens):
    t = np.arange(n_tokens)
    quarter = HDIM // 4
    inv = (ROPE_THETA ** (-np.arange(quarter, dtype=np.float32) / quarter)).astype(np.float32)
    rows = (t // GRID_W).astype(np.float32)[:, None] * inv
    cols = (t % GRID_W).astype(np.float32)[:, None] * inv
    ang = np.tile(np.concatenate([rows, cols], axis=-1), (1, LANES // (HDIM // 2)))
    return jnp.asarray(np.cos(ang), F32), jnp.asarray(np.sin(ang), F32)


def kernel(x, c, ctx, c_ctx, w_ada, b_ada, norm_g, w_in_ab, a_w_s, a_b_s, a_v_g, b_rpb, w_out_ab,
           w_qkv_c, c_q_g, c_k_g, w_out_c, w_up, conv_w, conv_b, w_down):
    bsz, seq, d = x.shape
    lc = ctx.shape[1]
    depth = w_ada.shape[0]
    aw = a_v_g.shape[-1]
    n_heads_b = b_rpb.shape[1]
    bw = n_heads_b * HDIM
    n_kv = 4
    n_heads_c = w_out_c.shape[1] // HDIM
    assert seq % (NA_QROWS * GRID_W) == 0 and seq // GRID_W >= NA_KROWS
    assert aw // A_GROUPS == A_CHUNK and w_up.shape[-1] // 2 % FFN_CW == 0

    n_rows = 8 * ((bsz + 1 + 7) // 8)
    cv = jnp.concatenate([c, c_ctx[None], jnp.zeros((n_rows - bsz - 1, d), F32)], axis=0)
    mods = _ada(cv, w_ada, b_ada).reshape(depth * n_rows * 6, 1, d)
    gains = norm_g.reshape(depth * 4, 1, d)

    def mod_vec(layer, j, is_ctx):
        base = layer * n_rows * 6 + j
        return mods, ((lambda bi: base + bsz * 6) if is_ctx else (lambda bi: base + bi * 6))

    def gain_vec(layer, j):
        return gains, (lambda bi: layer * 4 + j)

    merge = 2 if bsz % 2 == 0 and lc % A_CHUNK == 0 else 1
    merged = lambda a: a.reshape(bsz // merge, merge * lc, a.shape[-1])
    per_batch = lambda a: a.reshape(bsz, lc, a.shape[-1])

    cos_l, sin_l = _rope_tables(seq)
    cos_c, sin_c = jnp.ones((merge * lc, LANES), F32), jnp.zeros((merge * lc, LANES), F32)

    h, hc = x, merged(ctx)
    for layer in range(depth):
        with_ctx = layer < depth - 1
        lat = [mod_vec(layer, j, False) for j in range(6)]
        cx = [mod_vec(layer, j, True) for j in range(6)]
        g_pre_m, g_post_m, g_pre_f, g_post_f = [gain_vec(layer, j) for j in range(4)]
        if layer % 2 == 0:
            e = layer // 2
            w_in = w_in_ab[e].astype(BF16)
            w_s = a_w_s[e].astype(BF16)
            b_s_b = jnp.broadcast_to(a_b_s[e][:, :, None], a_w_s[e].shape[:2] + (aw // A_GROUPS,))
            g_v = a_v_g[e][None, :]
            w_out = w_out_ab[e].astype(BF16)
            ya, q, k, v = _inproj_ab(h, lat[0], lat[1], g_pre_m, w_in, w_s, b_s_b, g_v, aw, bw)
            yca, qc, kc, vc = _inproj_ab(hc, cx[0], cx[1], g_pre_m, w_in, w_s, b_s_b, g_v, aw, bw)
            qc, kc, vc = per_batch(qc), per_batch(kc), per_batch(vc)
            yb = _na_attention(q, k, v, kc, vc, _na_bias_table(b_rpb[e]))
            h = _outproj([ya, yb], [w_out[:aw], w_out[aw:]], h, lat[2], g_post_m)
            if with_ctx:
                hc = merged(_ctx_attention_outproj(qc, kc, vc, per_batch(yca), w_out[:aw], w_out[aw:],
                                                   per_batch(hc), cx[2], g_post_m))
        else:
            o = layer // 2
            nq = n_heads_c * HDIM
            nkv = n_kv * HDIM
            per = n_heads_c // n_kv
            half = HDIM // 2
            wqkv = w_qkv_c[o].astype(BF16)
            wq = wqkv[:, :nq].reshape(d, n_kv, per, 2, half).transpose(0, 2, 3, 1, 4).reshape(d, nq)
            wk = wqkv[:, nq:nq + nkv].reshape(d, n_kv, 2, half).transpose(0, 2, 1, 3).reshape(d, nkv)
            wv = wqkv[:, nq + nkv:]
            reps = LANES // half
            gq = (jnp.stack([jnp.tile(c_q_g[o][:half], reps), jnp.tile(c_q_g[o][half:], reps)])
                  * (HDIM ** -0.5 * LOG2E))
            gk = jnp.stack([jnp.tile(c_k_g[o][:half], reps), jnp.tile(c_k_g[o][half:], reps)])
            seg = np.arange(LANES) // half
            bd = jnp.asarray(seg[:, None] == seg[None, :], BF16)
            w_out = w_out_c[o].astype(BF16).reshape(n_kv, per, HDIM, d).transpose(1, 0, 2, 3).reshape(nq, d)
            q, k, v = _inproj_c(h, lat[0], lat[1], g_pre_m, wq, wk, wv, gq, gk, bd, cos_l, sin_l, True)
            kc, vc = _inproj_c(hc, cx[0], cx[1], g_pre_m, wq, wk, wv, gq, gk, bd, cos_c, sin_c, False)
            y = _gqa_attention(q, k, v, per_batch(kc), per_batch(vc))
            h = _outproj([y], [w_out], h, lat[2], g_post_m)
            if with_ctx:
                raise NotImplementedError("context update after a grouped-query layer")
        wts = _ffn_weights(w_up[layer], conv_w[layer], conv_b[layer], w_down[layer])
        h = _ffn(h, lat[3], lat[4], lat[5], g_pre_f, g_post_f, wts)
        if with_ctx:
            hc = _ffn(hc, cx[3], cx[4], cx[5], g_pre_f, g_post_f, wts, seq_len=lc)
    return h
```
